```python
import math
import jax, jax.numpy as jnp
from jax import lax
import numpy as np


D_MODEL = 1024
BATCH = 16
SEQ = 256
DEPTH = 2
DEC_BATCH = 8
DEC_SEQ = 4096
PAST_LEN = 256

GRID_W = 64
Q_BLOCK = 128
ROPE_THETA = 10000.0
EPS = 1e-6
NEG_BIG = -1e30

H_A = 4
HD_A = 64
H_B = 8
KV_B = 2
HD_B = 64
H_C = 8
Q_LORA = 384
KV_LORA = 256
NOPE_C = 64
ROPE_C = 32
V_C = 64
H_D = 8
KV_D = 2
HD_D = 64
WINDOW = 128

N_BRANCH = 4
BRANCH_W = 512

N_EXPERTS = 16
D_EXPERT = 1408
EC_FACTOR = 2

SCALE_A = HD_A ** -0.5
SCALE_B = HD_B ** -0.5
SCALE_C = (NOPE_C + ROPE_C) ** -0.5
SCALE_D = HD_D ** -0.5

SPLITS = (2 * H_A * HD_A, 2 * H_A * HD_A, H_A * 2 * HD_A,
          H_B * HD_B, KV_B * HD_B, KV_B * HD_B,
          Q_LORA, KV_LORA, ROPE_C,
          H_D * HD_D, KV_D * HD_D, KV_D * HD_D,
          N_BRANCH * D_MODEL)
IN_W = (2 * H_A * HD_A) * 3 + (H_B + 2 * KV_B) * HD_B + Q_LORA + KV_LORA + ROPE_C + (H_D + 2 * KV_D) * HD_D + N_BRANCH * D_MODEL

kernel_name = 'hybrid_diffusion_prefix_trunk_step'


def rms_norm(x, g):
    xf = x.astype(jnp.float32)
    y = xf * lax.rsqrt(jnp.mean(xf * xf, axis=-1, keepdims=True) + EPS)
    return (y * g.astype(jnp.float32)).astype(x.dtype)


def lambda_init(l):
    return 0.8 - 0.6 * math.exp(-0.3 * l)


def axial_rope(n_tokens, dim, dtype):
    rows = n_tokens // GRID_W
    row = jnp.repeat(jnp.arange(rows, dtype=jnp.float32), GRID_W)
    col = jnp.tile(jnp.arange(GRID_W, dtype=jnp.float32), rows)
    nf = dim // 4
    inv = ROPE_THETA ** (-jnp.arange(nf, dtype=jnp.float32) / nf)
    ar = row[:, None] * inv
    ac = col[:, None] * inv
    ang = jnp.concatenate([ar, ar, ac, ac], axis=-1)
    return jnp.cos(ang).astype(dtype), jnp.sin(ang).astype(dtype)


def apply_rope(x, cos, sin):
    nf = x.shape[-1] // 4
    xr = x.reshape(x.shape[:-1] + (2, 2, nf))
    rot = jnp.stack([-xr[..., 1, :], xr[..., 0, :]], axis=-2).reshape(x.shape)
    return x * cos[:, None, :] + rot * sin[:, None, :]


def modulation(cond, ada_w, ada_b):
    m = jnp.einsum('bd,de->be', jax.nn.silu(cond), ada_w) + ada_b
    return jnp.split(m[:, None, :], 6, axis=-1)


def sweep_query_blocks(fn, q):
    B, n = q.shape[:2]
    nb = n // Q_BLOCK
    qb = jnp.moveaxis(q.reshape((B, nb, Q_BLOCK) + q.shape[2:]), 1, 0)
    out = lax.map(lambda a: fn(a[0], a[1]), (jnp.arange(nb), qb))
    out = jnp.moveaxis(out, 0, 1)
    return out.reshape((B, n) + out.shape[3:])


def gqa_attend(qb, k, v, scale, sink=None, mask=None):
    B, Qb, H, dk = qb.shape
    G = k.shape[2]
    R = H // G
    q5 = qb.reshape(B, Qb, G, R, dk)
    s = jnp.einsum('bqgrd,bkgd->bgrqk', q5, k).astype(jnp.float32) * scale
    if mask is not None:
        s = jnp.where(mask, s, NEG_BIG)
    if sink is not None:
        sk = jnp.broadcast_to(sink.astype(jnp.float32).reshape(1, G, R, 1, 1), s.shape[:-1] + (1,))
        p = jax.nn.softmax(jnp.concatenate([s, sk], axis=-1), axis=-1)[..., :-1]
    else:
        p = jax.nn.softmax(s, axis=-1)
    o = jnp.einsum('bgrqk,bkgd->bqgrd', p.astype(v.dtype), v)
    return o.reshape(B, Qb, H, v.shape[-1])


def diff_attend(qb, k, v, lam):
    s = jnp.einsum('bqhd,bkhd->bhqk', qb, k).astype(jnp.float32) * SCALE_A
    p = jax.nn.softmax(s, axis=-1)
    B, H2, Qb, K = p.shape
    p = p.reshape(B, H2 // 2, 2, Qb, K)
    w = p[:, :, 0] - lam * p[:, :, 1]
    return jnp.einsum('bhqk,bkhe->bqhe', w.astype(v.dtype), v)


def diff_lambda(lam_p, l):
    lp = lam_p.astype(jnp.float32)
    return jnp.exp(jnp.sum(lp[0] * lp[1])) - jnp.exp(jnp.sum(lp[2] * lp[3])) + lambda_init(l)


def diff_finish(o, subln_g, l):
    B, n = o.shape[:2]
    return (rms_norm(o, subln_g) * (1.0 - lambda_init(l))).reshape(B, n, H_A * 2 * HD_A)


def window_attend_latent(q, k, v, k_ctx, v_ctx, sink):
    B, n = q.shape[:2]
    pad = ((0, 0), (Q_BLOCK, Q_BLOCK), (0, 0), (0, 0))
    kp = jnp.pad(k, pad)
    vp = jnp.pad(v, pad)
    r = jnp.arange(Q_BLOCK)[:, None]
    cidx = jnp.arange(3 * Q_BLOCK)[None, :]
    ctx_mask = jnp.ones((Q_BLOCK, k_ctx.shape[1]), dtype=bool)

    def blk(i, qb):
        start = i * Q_BLOCK
        kb = lax.dynamic_slice_in_dim(kp, start, 3 * Q_BLOCK, axis=1)
        vb = lax.dynamic_slice_in_dim(vp, start, 3 * Q_BLOCK, axis=1)
        kpos = start - Q_BLOCK + cidx
        qpos = start + r
        band = (jnp.abs(qpos - kpos) <= WINDOW) & (kpos >= 0) & (kpos < n)
        mask = jnp.concatenate([ctx_mask, band], axis=1)
        return gqa_attend(qb, jnp.concatenate([k_ctx, kb], axis=1), jnp.concatenate([v_ctx, vb], axis=1),
                          SCALE_D, sink=sink, mask=mask)

    return sweep_query_blocks(blk, q)


def mla_keys(c_kv, k_rope, w_kv_up, k_norm_g):
    B, n = c_kv.shape[:2]
    kv = jnp.einsum('bnr,re->bne', c_kv, w_kv_up).reshape(B, n, H_C, NOPE_C + V_C)
    k_nope = rms_norm(kv[..., :NOPE_C], k_norm_g)
    k = jnp.concatenate([k_nope, jnp.broadcast_to(k_rope[:, :, None, :], (B, n, H_C, ROPE_C))], axis=-1)
    return k, kv[..., NOPE_C:]


def mixer_inputs(h, P, l):
    B, n = h.shape[:2]
    z = jnp.einsum('bnd,de->bne', h, P['w_in'][l])
    cuts = [int(s) for s in np.cumsum(SPLITS)[:-1]]
    aq, ak, av, bq, bk, bv, cq, ckv, ckr, dq, dk, dv, gates = jnp.split(z, cuts, axis=-1)
    qc = jnp.einsum('bnr,re->bne', rms_norm(cq, P['c_qa_norm'][l]), P['c_wq_up'][l]).reshape(B, n, H_C, NOPE_C + ROPE_C)
    return {
        'a_q': rms_norm(aq.reshape(B, n, 2 * H_A, HD_A), P['a_q_norm'][l]),
        'a_k': rms_norm(ak.reshape(B, n, 2 * H_A, HD_A), P['a_k_norm'][l]),
        'a_v': av.reshape(B, n, H_A, 2 * HD_A),
        'b_q': rms_norm(bq.reshape(B, n, H_B, HD_B), P['b_q_norm'][l]),
        'b_k': rms_norm(bk.reshape(B, n, KV_B, HD_B), P['b_k_norm'][l]),
        'b_v': bv.reshape(B, n, KV_B, HD_B),
        'c_q_nope': rms_norm(qc[..., :NOPE_C], P['c_q_norm'][l]),
        'c_q_rope': rms_norm(qc[..., NOPE_C:], P['c_qr_norm'][l]),
        'c_kv': rms_norm(ckv, P['c_kva_norm'][l]),
        'c_kr': rms_norm(ckr, P['c_kr_norm'][l]),
        'd_q': rms_norm(dq.reshape(B, n, H_D, HD_D), P['d_q_norm'][l]),
        'd_k': rms_norm(dk.reshape(B, n, KV_D, HD_D), P['d_k_norm'][l]),
        'd_v': dv.reshape(B, n, KV_D, HD_D),
        'gates': gates,
    }


def merge_branches(outs, gates, w_branch, w_out):
    B, n = gates.shape[:2]
    g = jax.nn.sigmoid(gates.reshape(B, n, N_BRANCH, D_MODEL))
    merged = g[:, :, 0] * jnp.einsum('bnc,cd->bnd', outs[0], w_branch[0])
    for i in range(1, N_BRANCH):
        merged = merged + g[:, :, i] * jnp.einsum('bnc,cd->bnd', outs[i], w_branch[i])
    return jnp.einsum('bnd,de->bne', merged, w_out)


def context_mixer(h, P, l):
    B, n = h.shape[:2]
    f = mixer_inputs(h, P, l)
    lam = diff_lambda(P['a_lambda'][l], l)
    oa = sweep_query_blocks(lambda i, qb: diff_attend(qb, f['a_k'], f['a_v'], lam), f['a_q'])
    ob = sweep_query_blocks(lambda i, qb: gqa_attend(qb, f['b_k'], f['b_v'], SCALE_B), f['b_q'])
    kc, vc = mla_keys(f['c_kv'], f['c_kr'], P['c_wkv_up'][l], P['c_k_norm'][l])
    qc = jnp.concatenate([f['c_q_nope'], f['c_q_rope']], axis=-1)
    oc = sweep_query_blocks(lambda i, qb: gqa_attend(qb, kc, vc, SCALE_C), qc)
    sink = P['d_sink'][l]
    od = sweep_query_blocks(lambda i, qb: gqa_attend(qb, f['d_k'], f['d_v'], SCALE_D, sink=sink), f['d_q'])
    outs = (diff_finish(oa, P['a_subln'][l], l), ob.reshape(B, n, BRANCH_W),
            oc.reshape(B, n, BRANCH_W), od.reshape(B, n, BRANCH_W))
    y = merge_branches(outs, f['gates'], P['w_branch'][l], P['w_out'][l])
    cache = (f['a_k'], f['a_v'], f['b_k'], f['b_v'], f['c_kv'], f['c_kr'], f['d_k'], f['d_v'])
    return y, cache


def latent_mixer(h, ctx, P, l):
    ctx_ak, ctx_av, ctx_bk, ctx_bv, ctx_ckv, ctx_ckr, ctx_dk, ctx_dv = ctx
    B, n = h.shape[:2]
    dt = h.dtype
    f = mixer_inputs(h, P, l)
    rope_a = axial_rope(n, HD_A, dt)
    rope_b = axial_rope(n, HD_B, dt)
    rope_c = axial_rope(n, ROPE_C, dt)
    rope_d = axial_rope(n, HD_D, dt)
    lam = diff_lambda(P['a_lambda'][l], l)
    ka = jnp.concatenate([ctx_ak, apply_rope(f['a_k'], *rope_a)], axis=1)
    va = jnp.concatenate([ctx_av, f['a_v']], axis=1)
    oa = sweep_query_blocks(lambda i, qb: diff_attend(qb, ka, va, lam), apply_rope(f['a_q'], *rope_a))
    kb = jnp.concatenate([ctx_bk, apply_rope(f['b_k'], *rope_b)], axis=1)
    vb = jnp.concatenate([ctx_bv, f['b_v']], axis=1)
    ob = sweep_query_blocks(lambda i, qb: gqa_attend(qb, kb, vb, SCALE_B), apply_rope(f['b_q'], *rope_b))
    k_rope = apply_rope(f['c_kr'][:, :, None, :], *rope_c)[:, :, 0, :]
    kc_ctx, vc_ctx = mla_keys(ctx_ckv, ctx_ckr, P['c_wkv_up'][l], P['c_k_norm'][l])
    kc_lat, vc_lat = mla_keys(f['c_kv'], k_rope, P['c_wkv_up'][l], P['c_k_norm'][l])
    kc = jnp.concatenate([kc_ctx, kc_lat], axis=1)
    vc = jnp.concatenate([vc_ctx, vc_lat], axis=1)
    qc = jnp.concatenate([f['c_q_nope'], apply_rope(f['c_q_rope'], *rope_c)], axis=-1)
    oc = sweep_query_blocks(lambda i, qb: gqa_attend(qb, kc, vc, SCALE_C), qc)
    od = window_attend_latent(apply_rope(f['d_q'], *rope_d), apply_rope(f['d_k'], *rope_d), f['d_v'],
                              ctx_dk, ctx_dv, P['d_sink'][l])
    outs = (diff_finish(oa, P['a_subln'][l], l), ob.reshape(B, n, BRANCH_W),
            oc.reshape(B, n, BRANCH_W), od.reshape(B, n, BRANCH_W))
    return merge_branches(outs, f['gates'], P['w_branch'][l], P['w_out'][l])


def expert_choice_ffn(h, w_router, w_gate, w_up, w_down):
    B, n, D = h.shape
    x = h.reshape(B * n, D)
    cap = EC_FACTOR * (B * n) // N_EXPERTS
    aff = jax.nn.softmax(jnp.einsum('nd,de->ne', x, w_router).astype(jnp.float32), axis=-1)
    top_val, top_idx = lax.top_k(aff.T, cap)
    xe = x[top_idx]
    g = jnp.einsum('ecd,edf->ecf', xe, w_gate)
    u = jnp.einsum('ecd,edf->ecf', xe, w_up)
    ye = jnp.einsum('ecf,efd->ecd', jax.nn.silu(g) * u, w_down) * top_val[..., None].astype(x.dtype)
    y = jnp.zeros_like(x).at[top_idx.reshape(-1)].add(ye.reshape(-1, D))
    return y.reshape(B, n, D)


def trunk_layer(x, cond, P, l, ctx):
    sh1, sc1, g1, sh2, sc2, g2 = modulation(cond, P['ada_w'][l], P['ada_b'][l])
    h = rms_norm(x, P['norm1_g'][l]) * (1.0 + sc1) + sh1
    if ctx is None:
        mix, cache = context_mixer(h, P, l)
    else:
        mix = latent_mixer(h, ctx, P, l)
        cache = None
    x = x + g1 * mix
    h = rms_norm(x, P['norm2_g'][l]) * (1.0 + sc2) + sh2
    x = x + g2 * expert_choice_ffn(h, P['w_router'][l], P['w_gate'][l], P['w_up'][l], P['w_down'][l])
    return x, cache


def setup_inputs(seed: int = 0) -> dict:
    key = jax.random.key(seed)
    ks = iter(jax.random.split(key, 48))

    def nrm(shape, scale=1.0):
        return jax.random.normal(next(ks), shape, jnp.float32) * scale

    def gain(shape):
        return 1.0 + 0.1 * nrm(shape)

    L = DEPTH
    return {
        'x_prompt': nrm((BATCH, SEQ, D_MODEL)),
        'x_sample': nrm((DEC_BATCH, DEC_SEQ, D_MODEL)),
        'c': nrm((DEC_BATCH, D_MODEL)),
        'cache_a_k': nrm((DEC_BATCH, L, PAST_LEN, 2 * H_A, HD_A)),
        'cache_a_v': nrm((DEC_BATCH, L, PAST_LEN, H_A, 2 * HD_A)),
        'cache_b_k': nrm((DEC_BATCH, L, PAST_LEN, KV_B, HD_B)),
        'cache_b_v': nrm((DEC_BATCH, L, PAST_LEN, KV_B, HD_B)),
        'cache_c_kv': nrm((DEC_BATCH, L, PAST_LEN, KV_LORA)),
        'cache_c_kr': nrm((DEC_BATCH, L, PAST_LEN, ROPE_C)),
        'cache_d_k': nrm((DEC_BATCH, L, PAST_LEN, KV_D, HD_D)),
        'cache_d_v': nrm((DEC_BATCH, L, PAST_LEN, KV_D, HD_D)),
        'c_ctx': nrm((D_MODEL,)),
        'ada_w': nrm((L, D_MODEL, 6 * D_MODEL), 0.5 * D_MODEL ** -0.5),
        'ada_b': nrm((L, 6 * D_MODEL), 0.02),
        'norm1_g': gain((L, D_MODEL)),
        'norm2_g': gain((L, D_MODEL)),
        'w_in': nrm((L, D_MODEL, IN_W), D_MODEL ** -0.5),
        'a_q_norm': gain((L, HD_A)),
        'a_k_norm': gain((L, HD_A)),
        'a_lambda': nrm((L, 4, HD_A), 0.1),
        'a_subln': gain((L, 2 * HD_A)),
        'b_q_norm': gain((L, HD_B)),
        'b_k_norm': gain((L, HD_B)),
        'c_qa_norm': gain((L, Q_LORA)),
        'c_kva_norm': gain((L, KV_LORA)),
        'c_wq_up': nrm((L, Q_LORA, H_C * (NOPE_C + ROPE_C)), Q_LORA ** -0.5),
        'c_wkv_up': nrm((L, KV_LORA, H_C * (NOPE_C + V_C)), KV_LORA ** -0.5),
        'c_q_norm': gain((L, NOPE_C)),
        'c_k_norm': gain((L, NOPE_C)),
        'c_qr_norm': gain((L, ROPE_C)),
        'c_kr_norm': gain((L, ROPE_C)),
        'd_q_norm': gain((L, HD_D)),
        'd_k_norm': gain((L, HD_D)),
        'd_sink': nrm((L, H_D), 0.5),
        'w_branch': nrm((L, N_BRANCH, BRANCH_W, D_MODEL), BRANCH_W ** -0.5),
        'w_out': nrm((L, D_MODEL, D_MODEL), D_MODEL ** -0.5),
        'w_router': nrm((L, D_MODEL, N_EXPERTS), D_MODEL ** -0.5),
        'w_gate': nrm((L, N_EXPERTS, D_MODEL, D_EXPERT), D_MODEL ** -0.5),
        'w_up': nrm((L, N_EXPERTS, D_MODEL, D_EXPERT), D_MODEL ** -0.5),
        'w_down': nrm((L, N_EXPERTS, D_EXPERT, D_MODEL), D_EXPERT ** -0.5),
    }


def reference(x_prompt, x_sample, c, cache_a_k, cache_a_v, cache_b_k, cache_b_v, cache_c_kv, cache_c_kr,
              cache_d_k, cache_d_v, c_ctx, ada_w, ada_b, norm1_g, norm2_g, w_in, a_q_norm, a_k_norm,
              a_lambda, a_subln, b_q_norm, b_k_norm, c_qa_norm, c_kva_norm, c_wq_up, c_wkv_up, c_q_norm,
              c_k_norm, c_qr_norm, c_kr_norm, d_q_norm, d_k_norm, d_sink, w_branch, w_out, w_router,
              w_gate, w_up, w_down):
    P = {
        'ada_w': ada_w, 'ada_b': ada_b, 'norm1_g': norm1_g, 'norm2_g': norm2_g, 'w_in': w_in,
        'a_q_norm': a_q_norm, 'a_k_norm': a_k_norm, 'a_lambda': a_lambda, 'a_subln': a_subln,
        'b_q_norm': b_q_norm, 'b_k_norm': b_k_norm,
        'c_qa_norm': c_qa_norm, 'c_kva_norm': c_kva_norm, 'c_wq_up': c_wq_up, 'c_wkv_up': c_wkv_up,
        'c_q_norm': c_q_norm, 'c_k_norm': c_k_norm, 'c_qr_norm': c_qr_norm, 'c_kr_norm': c_kr_norm,
        'd_q_norm': d_q_norm, 'd_k_norm': d_k_norm, 'd_sink': d_sink,
        'w_branch': w_branch, 'w_out': w_out,
        'w_router': w_router, 'w_gate': w_gate, 'w_up': w_up, 'w_down': w_down,
    }
    x = x_prompt
    layer_caches = []
    for l in range(DEPTH):
        x, cache = trunk_layer(x, c_ctx[None, :], P, l, None)
        layer_caches.append(cache)
    y_prompt = x
    new_a_k, new_a_v, new_b_k, new_b_v, new_c_kv, new_c_kr, new_d_k, new_d_v = (
        jnp.stack([lc[j] for lc in layer_caches], axis=1) for j in range(8))
    x = x_sample
    for l in range(DEPTH):
        ctx = (cache_a_k[:, l], cache_a_v[:, l], cache_b_k[:, l], cache_b_v[:, l],
               cache_c_kv[:, l], cache_c_kr[:, l], cache_d_k[:, l], cache_d_v[:, l])
        x, _ = trunk_layer(x, c, P, l, ctx)
    y_sample = x
    return (y_prompt, y_sample, new_a_k, new_a_v, new_b_k, new_b_v, new_c_kv, new_c_kr, new_d_k, new_d_v)
```

```python
import functools
import math

import numpy as np
import jax
import jax.numpy as jnp
from jax import lax
from jax.experimental import pallas as pl
from jax.experimental.pallas import tpu as pltpu

F32 = jnp.float32
BF16 = jnp.bfloat16
I32 = jnp.int32

D_MODEL = 1024
GRID_W = 64
ROPE_THETA = 10000.0
EPS = 1e-6
NEG_BIG = -1e30
LOG2E = 1.4426950408889634

H_A, HD_A = 4, 64
H_B, KV_B, HD_B = 8, 2, 64
H_C, Q_LORA, KV_LORA, NOPE_C, ROPE_C, V_C = 8, 384, 256, 64, 32, 64
H_D, KV_D, HD_D = 8, 2, 64
WINDOW = 128
N_BRANCH, BRANCH_W = 4, 512
N_EXPERTS, D_EXPERT, EC_FACTOR = 16, 1408, 2

SCALE_A = HD_A ** -0.5
SCALE_B = HD_B ** -0.5
SCALE_C = (NOPE_C + ROPE_C) ** -0.5
SCALE_D = HD_D ** -0.5

LANES_V7X = 128
VMEM_LIMIT_BYTES_V7X = 56 * 1024 * 1024

_W_ORIG_NOGATE = 3744
_C_AQ, _C_AK, _C_AV = (0, 512), (512, 1024), (1024, 1536)
_C_BQ, _C_BK, _C_BV = (1536, 2048), (2048, 2176), (2176, 2304)
_C_CQ, _C_CKV, _C_CKR = (2304, 2688), (2688, 2944), (2944, 3072)
_C_DQ, _C_DK, _C_DV = (3072, 3584), (3584, 3712), (3712, 3840)
_W1_COLS = 3840
_CKR_END_ORIG = 2976

HALF_BITS = 16
WORDS_PER_TILE = LANES_V7X // HALF_BITS


def _lambda_init(l):
    return 0.8 - 0.6 * math.exp(-0.3 * l)


def _cparams(n_axes):
    return pltpu.CompilerParams(
        dimension_semantics=("arbitrary",) * n_axes,
        vmem_limit_bytes=VMEM_LIMIT_BYTES_V7X,
    )


def _const_spec(shape):
    nd = len(shape)
    return pl.BlockSpec(shape, lambda *_: (0,) * nd)


def _inv_rms(z):
    return lax.rsqrt(jnp.mean(z * z, axis=-1, keepdims=True) + EPS)


def _sigmoid(x):
    return 1.0 / (1.0 + jnp.exp(-x))


def _mod_body(c_ref, w_ref, b_ref, o_ref):
    c = c_ref[...]
    s = (c * _sigmoid(c)).astype(BF16)
    o_ref[0] = jnp.dot(s, w_ref[0].astype(BF16), preferred_element_type=F32) + b_ref[0]


def _modulation(cond, ada_w, ada_b):
    L = ada_w.shape[0]
    R = cond.shape[0]
    nblk = 6
    return pl.pallas_call(
        _mod_body,
        out_shape=jax.ShapeDtypeStruct((L, R, 6 * D_MODEL), F32),
        grid=(L, nblk),
        in_specs=[
            pl.BlockSpec((R, D_MODEL), lambda l, j: (0, 0)),
            pl.BlockSpec((1, D_MODEL, D_MODEL), lambda l, j: (l, 0, j)),
            pl.BlockSpec((1, 1, D_MODEL), lambda l, j: (l, 0, j)),
        ],
        out_specs=pl.BlockSpec((1, R, D_MODEL), lambda l, j: (l, 0, j)),
        compiler_params=_cparams(2),
        name="modulation",
    )(cond, ada_w, ada_b.reshape(L, 1, 6 * D_MODEL))


def _ada_norm(x, g, shift, scale):
    return (x * _inv_rms(x) * g) * (1.0 + scale) + shift


def _rotary(zg, cos, sin_a, sin_b, nf):
    w = zg.shape[-1]
    return zg * cos + pltpu.roll(zg, w - nf, 1) * sin_a + pltpu.roll(zg, nf, 1) * sin_b


def _tile_lanes(t, width):
    reps = width // t.shape[-1]
    return t if reps == 1 else jnp.concatenate([t] * reps, axis=1)


def _head_norm(z, g_tiled, n_heads, hd, rope):
    zg = z * g_tiled
    if rope is not None:
        cos, sin_a, sin_b, nf = rope
        w = z.shape[-1]
        zg = _rotary(zg, _tile_lanes(cos, w), _tile_lanes(sin_a, w), _tile_lanes(sin_b, w), nf)
    outs = []
    for h in range(n_heads):
        zh = z[:, h * hd:(h + 1) * hd]
        outs.append(zg[:, h * hd:(h + 1) * hd] * _inv_rms(zh))
    return outs


def _mla_expand(ckv_n, kr, wkv_ref, ckn_tiled, kc_ref, vc_ref):
    kv = jnp.dot(ckv_n.astype(BF16), wkv_ref[...], preferred_element_type=F32)
    kn = _head_norm(kv[:, :H_C * NOPE_C], ckn_tiled, H_C, NOPE_C, None)
    krb = kr.astype(BF16)
    t = kv.shape[0]
    pad = jnp.zeros((t, LANES_V7X - NOPE_C - ROPE_C), BF16)
    for h in range(H_C):
        kc_ref[0, h, :, 0:NOPE_C] = kn[h].astype(BF16)
        kc_ref[0, h, :, NOPE_C:NOPE_C + ROPE_C] = krb
        kc_ref[0, h, :, NOPE_C + ROPE_C:LANES_V7X] = pad
        vc_ref[0, h] = kv[:, H_C * NOPE_C + h * V_C:H_C * NOPE_C + (h + 1) * V_C].astype(BF16)


def _proj_body(rope, emit_cache, *refs):
    it = iter(refs)
    x_ref, mod_ref, g1_ref, w1_ref = next(it), next(it), next(it), next(it)
    aqn, akn, bqn, bkn = next(it), next(it), next(it), next(it)
    cqan, ckvan, wqup_ref, wkvup_ref = next(it), next(it), next(it), next(it)
    cqn, ckn, cqrn, ckrn, dqn, dkn = next(it), next(it), next(it), next(it), next(it), next(it)
    if rope:
        cos64, sa64, sb64, cos32, sa32, sb32 = (next(it)[...] for _ in range(6))
        rope64 = (cos64, sa64, sb64, HD_A // 4)
        rope32 = (cos32, sa32, sb32, ROPE_C // 4)
    else:
        rope64 = rope32 = None
    qa_ref, ka_ref, va_ref = next(it), next(it), next(it)
    qb_ref, kb_ref, vb_ref = next(it), next(it), next(it)
    qc_ref, kc_ref, vc_ref = next(it), next(it), next(it)
    qd_ref, kd_ref, vd_ref = next(it), next(it), next(it)
    if emit_cache:
        c_ak, c_av, c_bk, c_bv, c_ckv, c_ckr, c_dk, c_dv = (next(it) for _ in range(8))

    x = x_ref[0]
    m = mod_ref[0]
    hb = _ada_norm(x, g1_ref[...], m[:, 0:D_MODEL], m[:, D_MODEL:2 * D_MODEL]).astype(BF16)
    t = x.shape[0]

    def seg(c):
        return jnp.dot(hb, w1_ref[:, c[0]:c[1]], preferred_element_type=F32)

    def store_heads(ys, ref):
        for h, y in enumerate(ys):
            ref[0, h] = y.astype(BF16)

    def store_cache(ys, ref, hd):
        for h, y in enumerate(ys):
            ref[0, :, h * hd:(h + 1) * hd] = y

    store_heads(_head_norm(seg(_C_AQ), aqn[...], 2 * H_A, HD_A, rope64), qa_ref)
    ys = _head_norm(seg(_C_AK), akn[...], 2 * H_A, HD_A, rope64)
    store_heads(ys, ka_ref)
    if emit_cache:
        store_cache(ys, c_ak, HD_A)
    av = seg(_C_AV)
    for h in range(H_A):
        va_ref[0, h] = av[:, h * 2 * HD_A:(h + 1) * 2 * HD_A].astype(BF16)
    if emit_cache:
        c_av[0] = av

    for (cq, ck, cv, qn, kn_, q_ref, k_ref, v_ref, ck_ref, cv_ref) in (
        (_C_BQ, _C_BK, _C_BV, bqn, bkn, qb_ref, kb_ref, vb_ref,
         c_bk if emit_cache else None, c_bv if emit_cache else None),
        (_C_DQ, _C_DK, _C_DV, dqn, dkn, qd_ref, kd_ref, vd_ref,
         c_dk if emit_cache else None, c_dv if emit_cache else None),
    ):
        store_heads(_head_norm(seg(cq), qn[...], H_B, HD_B, rope64), q_ref)
        ys = _head_norm(seg(ck), kn_[...], KV_B, HD_B, rope64)
        store_heads(ys, k_ref)
        v = seg(cv)
        for h in range(KV_B):
            v_ref[0, h] = v[:, h * HD_B:(h + 1) * HD_B].astype(BF16)
        if emit_cache:
            store_cache(ys, ck_ref, HD_B)
            cv_ref[0] = v

    cq = seg(_C_CQ)
    cq_n = (cq * _inv_rms(cq) * cqan[...]).astype(BF16)
    zq = jnp.dot(cq_n, wqup_ref[...], preferred_element_type=F32)
    q_nope = _head_norm(zq[:, :H_C * NOPE_C], cqn[...], H_C, NOPE_C, None)
    q_rope = _head_norm(zq[:, H_C * NOPE_C:], cqrn[...], H_C, ROPE_C, rope32)
    pad = jnp.zeros((t, LANES_V7X - NOPE_C - ROPE_C), BF16)
    for h in range(H_C):
        qc_ref[0, h, :, 0:NOPE_C] = q_nope[h].astype(BF16)
        qc_ref[0, h, :, NOPE_C:NOPE_C + ROPE_C] = q_rope[h].astype(BF16)
        qc_ref[0, h, :, NOPE_C + ROPE_C:LANES_V7X] = pad

    ckv = seg(_C_CKV)
    ckv_n = ckv * _inv_rms(ckv) * ckvan[...]
    ckr = seg(_C_CKR)
    kr_plain = ckr * ckrn[...] * _inv_rms(ckr[:, :ROPE_C])
    if emit_cache:
        c_ckv[0] = ckv_n
        c_ckr[0] = kr_plain[:, :ROPE_C]
    if rope:
        zg = ckr * ckrn[...]
        kr = _rotary(zg, cos32, sa32, sb32, ROPE_C // 4) * _inv_rms(ckr[:, :ROPE_C])
    else:
        kr = kr_plain
    _mla_expand(ckv_n, kr[:, :ROPE_C], wkvup_ref, ckn[...], kc_ref, vc_ref)


def _proj(x, mod, P, l, rope_tabs, emit_cache, t_tile):
    B, n, _ = x.shape
    nt = n // t_tile
    mod_b = mod.shape[0]
    grid = (B, nt)

    def tok_spec(w):
        return pl.BlockSpec((1, t_tile, w), lambda b, i: (b, i, 0))

    def head_spec(h, d):
        return pl.BlockSpec((1, h, t_tile, d), lambda b, i: (b, 0, i, 0))

    small = [P["aqn"][l], P["akn"][l], P["bqn"][l], P["bkn"][l], P["cqan"][l], P["ckvan"][l],
             P["wqup"][l], P["wkvup"][l], P["cqn"][l], P["ckn"][l], P["cqrn"][l], P["ckrn"][l],
             P["dqn"][l], P["dkn"][l]]
    ins = [x, mod, P["g1"][l], P["w1"][l]] + small
    in_specs = [
        tok_spec(D_MODEL),
        pl.BlockSpec((1, 1, 6 * D_MODEL), (lambda b, i: (b, 0, 0)) if mod_b > 1 else (lambda b, i: (0, 0, 0))),
        _const_spec((1, D_MODEL)),
        pl.BlockSpec((D_MODEL, _W1_COLS), lambda b, i: (0, 0), pipeline_mode=pl.Buffered(1)),
    ] + [_const_spec(a.shape) for a in small]
    rope = rope_tabs is not None
    if rope:
        ins += list(rope_tabs)
        in_specs += [pl.BlockSpec((t_tile, LANES_V7X), lambda b, i: (i, 0)) for _ in rope_tabs]

    def hs(h, d):
        return jax.ShapeDtypeStruct((B, h, n, d), BF16)

    out_shape = [hs(8, 64), hs(8, 64), hs(4, 128), hs(8, 64), hs(2, 64), hs(2, 64),
                 hs(8, 128), hs(8, 128), hs(8, 64), hs(8, 64), hs(2, 64), hs(2, 64)]
    out_specs = [head_spec(8, 64), head_spec(8, 64), head_spec(4, 128), head_spec(8, 64), head_spec(2, 64),
                 head_spec(2, 64), head_spec(8, 128), head_spec(8, 128), head_spec(8, 64), head_spec(8, 64),
                 head_spec(2, 64), head_spec(2, 64)]
    if emit_cache:
        for w in (512, 512, 128, 128, KV_LORA, ROPE_C, 128, 128):
            out_shape.append(jax.ShapeDtypeStruct((B, n, w), F32))
            out_specs.append(tok_spec(w))
    return pl.pallas_call(
        functools.partial(_proj_body, rope, emit_cache),
        out_shape=out_shape, grid=grid, in_specs=in_specs, out_specs=out_specs,
        compiler_params=_cparams(2), name="proj",
    )(*ins)


def _mla_cache_body(ckv_ref, ckr_ref, wkvup_ref, ckn_ref, kc_ref, vc_ref):
    _mla_expand(ckv_ref[0], ckr_ref[0], wkvup_ref, ckn_ref[...], kc_ref, vc_ref)


def _mla_cache(ckv, ckr, wkvup, ckn_tiled):
    B, K, _ = ckv.shape
    return pl.pallas_call(
        _mla_cache_body,
        out_shape=[jax.ShapeDtypeStruct((B, H_C, K, LANES_V7X), BF16),
                   jax.ShapeDtypeStruct((B, H_C, K, V_C), BF16)],
        grid=(B,),
        in_specs=[pl.BlockSpec((1, K, KV_LORA), lambda b: (b, 0, 0)),
                  pl.BlockSpec((1, K, ROPE_C), lambda b: (b, 0, 0)),
                  _const_spec(wkvup.shape), _const_spec(ckn_tiled.shape)],
        out_specs=[pl.BlockSpec((1, H_C, K, LANES_V7X), lambda b: (b, 0, 0, 0)),
                   pl.BlockSpec((1, H_C, K, V_C), lambda b: (b, 0, 0, 0))],
        compiler_params=_cparams(1), name="mla_cache",
    )(ckv, ckr, wkvup, ckn_tiled)


def _attn_body(cfg, *refs):
    it = iter(refs)
    q_ref = next(it)
    kc_ref = vc_ref = None
    if cfg["has_ctx"]:
        kc_ref, vc_ref = next(it), next(it)
    ko_ref, vo_ref = next(it), next(it)
    sink_ref = next(it) if cfg["sink"] else None
    if cfg["diff"] is not None:
        lam_ref, subln_ref = next(it), next(it)
    o_ref = next(it)

    R, RK, RV = cfg["R"], cfg["RK"], cfg["RV"]
    tq, tk, ko = cfg["tq"], cfg["tk"], cfg["ko"]
    scale = cfg["scale"]
    c = scale * LOG2E
    dv = vo_ref.shape[-1]
    g = pl.program_id(1)
    i = pl.program_id(2)
    nt_dims = (((1,), (1,)), ((), ()))

    outs = []
    for r in range(R):
        rk = r if RK > 1 else 0
        rv = r if RV > 1 else 0
        q = q_ref[0, r]

        def step(k, v, carry, mask=None, q=q):
            m, l, acc = carry
            s = lax.dot_general(q, k, nt_dims, preferred_element_type=F32)
            if mask is not None:
                s = jnp.where(mask, s, NEG_BIG)
            m_new = jnp.maximum(m, jnp.max(s, axis=-1, keepdims=True))
            alpha = jnp.exp2((m - m_new) * c)
            p = jnp.exp2((s - m_new) * c)
            l = alpha * l + jnp.sum(p, axis=-1, keepdims=True)
            acc = alpha * acc + jnp.dot(p.astype(BF16), v, preferred_element_type=F32)
            return m_new, l, acc

        if cfg["sink"]:
            m0 = jnp.full((tq, 1), sink_ref[g * R + r] / scale, F32)
            l0 = jnp.ones((tq, 1), F32)
        else:
            m0 = jnp.full((tq, 1), NEG_BIG, F32)
            l0 = jnp.zeros((tq, 1), F32)
        carry = (m0, l0, jnp.zeros((tq, dv), F32))
        if cfg["has_ctx"]:
            carry = step(kc_ref[0, rk], vc_ref[0, rv], carry)
        if cfg["window"] is not None:
            win = cfg["window"]
            wk = min(tq + 2 * win, ko)
            q0 = i * tq
            k0 = pl.multiple_of(jnp.clip(q0 - win, 0, ko - wk), win)
            qpos = q0 + lax.broadcasted_iota(I32, (tq, wk), 0)
            kpos = k0 + lax.broadcasted_iota(I32, (tq, wk), 1)
            mask = jnp.abs(qpos - kpos) <= win
            carry = step(ko_ref[0, rk, pl.ds(k0, wk), :], vo_ref[0, rv, pl.ds(k0, wk), :], carry, mask)
        elif ko == tk:
            carry = step(ko_ref[0, rk], vo_ref[0, rv], carry)
        else:
            def body(j, carry, rk=rk, rv=rv, step=step):
                st = pl.multiple_of(j * tk, tk)
                return step(ko_ref[0, rk, pl.ds(st, tk), :], vo_ref[0, rv, pl.ds(st, tk), :], carry)
            carry = lax.fori_loop(0, ko // tk, body, carry)
        _, l, acc = carry
        outs.append(acc / l)

    if cfg["diff"] is not None:
        lp = lam_ref[...]
        lam = (jnp.exp(jnp.sum(lp[0:1] * lp[1:2], axis=-1, keepdims=True))
               - jnp.exp(jnp.sum(lp[2:3] * lp[3:4], axis=-1, keepdims=True)) + cfg["diff"])
        d = outs[0] - lam * outs[1]
        o_ref[0] = (d * _inv_rms(d) * subln_ref[...] * (1.0 - cfg["diff"])).astype(BF16)
    else:
        for r in range(R):
            o_ref[0, :, r * dv:(r + 1) * dv] = outs[r].astype(BF16)


def _attention(q, ko, vo, kc, vc, *, scale, kmap, vmap, tq, tk, window=None, sink=None, diff=None,
               lam=None, subln=None, name="attn"):
    B, hq, n, dk = q.shape
    R = 2
    k_heads = ko.shape[1]
    v_heads = vo.shape[1]
    RK = R if k_heads == hq else 1
    RV = R if v_heads == hq else 1
    k_own = ko.shape[2]
    dv = vo.shape[-1]
    has_ctx = kc is not None
    cfg = dict(R=R, RK=RK, RV=RV, tq=tq, tk=min(tk, k_own), ko=k_own, scale=scale, has_ctx=has_ctx,
               window=window, sink=sink is not None, diff=diff)
    grid = (B, hq // R, n // tq)
    ins = [q]
    in_specs = [pl.BlockSpec((1, R, tq, dk), lambda b, g, i: (b, g, i, 0))]

    def kv_spec(arr, r, hmap):
        return pl.BlockSpec((1, r, arr.shape[2], arr.shape[3]), lambda b, g, i: (b, hmap(g), 0, 0))

    if has_ctx:
        ins += [kc, vc]
        in_specs += [kv_spec(kc, RK, kmap), kv_spec(vc, RV, vmap)]
    ins += [ko, vo]
    in_specs += [kv_spec(ko, RK, kmap), kv_spec(vo, RV, vmap)]
    if sink is not None:
        ins.append(sink)
        in_specs.append(pl.BlockSpec(memory_space=pltpu.SMEM))
    if diff is not None:
        ins += [lam, subln]
        in_specs += [_const_spec(lam.shape), _const_spec(subln.shape)]
    out_w = (hq // R) * LANES_V7X
    return pl.pallas_call(
        functools.partial(_attn_body, cfg),
        out_shape=jax.ShapeDtypeStruct((B, n, out_w), BF16),
        grid=grid, in_specs=in_specs,
        out_specs=pl.BlockSpec((1, tq, LANES_V7X), lambda b, g, i: (b, i, g)),
        compiler_params=_cparams(3), name=name,
    )(*ins)


def _merge_body(x_ref, mod_ref, g1_ref, g2_ref, oa_ref, ob_ref, oc_ref, od_ref, wg_ref, wb_ref, wo_ref,
                wr_ref, wrt_ref, x1_ref, h2_ref, aff3_ref, afft_ref):
    x = x_ref[0]
    m = mod_ref[0]
    D = D_MODEL
    hb = _ada_norm(x, g1_ref[...], m[:, 0:D], m[:, D:2 * D]).astype(BF16)
    merged = None
    for bi, o_ref in enumerate((oa_ref, ob_ref, oc_ref, od_ref)):
        gate = _sigmoid(jnp.dot(hb, wg_ref[:, bi * D:(bi + 1) * D], preferred_element_type=F32))
        br = jnp.dot(o_ref[0], wb_ref[bi], preferred_element_type=F32)
        merged = gate * br if merged is None else merged + gate * br
    mix = jnp.dot(merged.astype(BF16), wo_ref[...], preferred_element_type=F32)
    x1 = x + m[:, 2 * D:3 * D] * mix
    x1_ref[0] = x1
    h2 = _ada_norm(x1, g2_ref[...], m[:, 3 * D:4 * D], m[:, 4 * D:5 * D])
    h2_ref[0] = h2
    h2b = h2.astype(BF16)
    lg = jnp.dot(h2b, wr_ref[...], preferred_element_type=F32)
    e = jnp.exp(lg - jnp.max(lg, axis=-1, keepdims=True))
    afft_ref[0] = e / jnp.sum(e, axis=-1, keepdims=True)
    lt = lax.dot_general(wrt_ref[...], h2b, (((1,), (1,)), ((), ())), preferred_element_type=F32)
    et = jnp.exp(lt - jnp.max(lt, axis=0, keepdims=True))
    at = et / jnp.sum(et, axis=0, keepdims=True)
    for ti in range(x.shape[0] // LANES_V7X):
        aff3_ref[0, ti] = at[:, ti * LANES_V7X:(ti + 1) * LANES_V7X]


def _merge(x, mod, P, l, outs, t_tile):
    B, n, D = x.shape
    nt = n // t_tile
    mod_b = mod.shape[0]
    tl = t_tile // LANES_V7X

    def tok_spec(w):
        return pl.BlockSpec((1, t_tile, w), lambda b, i: (b, i, 0))

    def w_spec(shape):
        nd = len(shape)
        return pl.BlockSpec(shape, lambda b, i: (0,) * nd, pipeline_mode=pl.Buffered(1))

    ins = [x, mod, P["g1"][l], P["g2"][l], *outs, P["wgates"][l], P["wbranch"][l], P["wout"][l],
           P["wrouter"][l], P["wrouter_t"][l]]
    in_specs = [
        tok_spec(D),
        pl.BlockSpec((1, 1, 6 * D), (lambda b, i: (b, 0, 0)) if mod_b > 1 else (lambda b, i: (0, 0, 0))),
        _const_spec((1, D)), _const_spec((1, D)),
        tok_spec(BRANCH_W), tok_spec(BRANCH_W), tok_spec(BRANCH_W), tok_spec(BRANCH_W),
        w_spec((D, N_BRANCH * D)), w_spec((N_BRANCH, BRANCH_W, D)), w_spec((D, D)),
        _const_spec((D, N_EXPERTS)), _const_spec((N_EXPERTS, D)),
    ]
    out_shape = [
        jax.ShapeDtypeStruct((B, n, D), F32),
        jax.ShapeDtypeStruct((B, n, D), F32),
        jax.ShapeDtypeStruct((B, n // LANES_V7X, N_EXPERTS, LANES_V7X), F32),
        jax.ShapeDtypeStruct((B, n, N_EXPERTS), F32),
    ]
    out_specs = [
        tok_spec(D), tok_spec(D),
        pl.BlockSpec((1, tl, N_EXPERTS, LANES_V7X), lambda b, i: (b, i, 0, 0)),
        tok_spec(N_EXPERTS),
    ]
    return pl.pallas_call(
        _merge_body, out_shape=out_shape, grid=(B, nt), in_specs=in_specs, out_specs=out_specs,
        compiler_params=_cparams(2), name="merge",
    )(*ins)


def _route_body(cap, aff_ref, tri_ref, ones_ref, wbits_ref, words_ref, starts_ref, tot_sc, off_sc):
    nt = aff_ref.shape[0]
    E = N_EXPERTS
    bits = pltpu.bitcast(aff_ref[...], I32)

    def count_ge(th):
        c = jnp.sum(jnp.where(bits >= th[None], 1, 0), axis=0)
        return jnp.broadcast_to(jnp.sum(c, axis=1, keepdims=True), (E, LANES_V7X))

    def bisect(_, carry):
        lo, hi = carry
        mid = lo + ((hi - lo) >> 1)
        ok = count_ge(mid) >= cap
        return jnp.where(ok, mid, lo), jnp.where(ok, hi, mid)

    lo0 = jnp.zeros((E, LANES_V7X), I32)
    hi0 = jnp.full((E, LANES_V7X), 0x7F800000, I32)
    thr, _ = lax.fori_loop(0, 31, bisect, (lo0, hi0))

    gt = bits > thr[None]
    eq = bits == thr[None]
    gt_f = jnp.where(gt, 1.0, 0.0)
    eq_f = jnp.where(eq, 1.0, 0.0)
    n_gt = jnp.sum(jnp.sum(gt_f, axis=0), axis=1, keepdims=True)
    need = jnp.broadcast_to(float(cap) - n_gt, (E, LANES_V7X))[None]

    def prefix(mask_f):
        m2 = mask_f.astype(BF16).reshape(nt * E, LANES_V7X)
        incl = jnp.dot(m2, tri_ref[...], preferred_element_type=F32).reshape(nt, E, LANES_V7X)
        tot_sc[...] = jnp.dot(m2, ones_ref[...], preferred_element_type=F32).reshape(nt, E, LANES_V7X)

        def scan(i, run):
            off_sc[i] = run
            return run + tot_sc[i]
        lax.fori_loop(0, nt, scan, jnp.zeros((E, LANES_V7X), F32))
        offs = off_sc[...]
        return incl - mask_f + offs, offs

    ex_gt, off_gt = prefix(gt_f)
    ex_eq, off_eq = prefix(eq_f)
    sel = jnp.logical_or(gt, jnp.logical_and(eq, ex_eq < need))
    sel2 = jnp.where(sel, 1.0, 0.0).astype(BF16).reshape(nt * E, LANES_V7X)
    words = jnp.dot(sel2, wbits_ref[...], preferred_element_type=F32)
    words_ref[...] = words.astype(I32).reshape(nt, E, LANES_V7X)
    starts_ref[...] = (off_gt + jnp.minimum(off_eq, need)).astype(I32)


def _route(aff3, cap):
    nt = aff3.shape[0]
    j = np.arange(LANES_V7X)
    tri = jnp.asarray(j[:, None] <= j[None, :], BF16)
    ones = jnp.ones((LANES_V7X, LANES_V7X), BF16)
    wb = np.zeros((LANES_V7X, LANES_V7X), np.float32)
    wb[j, j // HALF_BITS] = 2.0 ** (j % HALF_BITS)
    shp = (nt, N_EXPERTS, LANES_V7X)
    return pl.pallas_call(
        functools.partial(_route_body, cap),
        out_shape=[jax.ShapeDtypeStruct(shp, I32), jax.ShapeDtypeStruct(shp, I32)],
        scratch_shapes=[pltpu.VMEM(shp, F32), pltpu.VMEM(shp, F32)],
        compiler_params=pltpu.CompilerParams(vmem_limit_bytes=VMEM_LIMIT_BYTES_V7X),
        name="route",
    )(aff3, tri, ones, jnp.asarray(wb, BF16))


def _lowbit_lut():
    lut = np.zeros((256,), np.int32)
    for b in range(1, 256):
        lut[b] = (b & -b).bit_length() - 1
    return jnp.asarray(lut)


def _lowest_bit(cur, lut_ref):
    lo = cur & 0xFF
    return jnp.where(lo != 0, lut_ref[lo], 8 + lut_ref[(cur >> 8) & 0xFF])


def _ffn_body(ts, kt, tab_ref, tabn_ref, lut_ref, h_hbm, wg_ref, wu_ref, wd_ref, o_ref, xbuf, sem, st):
    e = pl.program_id(0)
    k = pl.program_id(1)
    n_e = pl.num_programs(0)
    s = e * kt + k
    slot = s % 2

    def row_copy(tok, buf_slot, j):
        return pltpu.make_async_copy(h_hbm.at[pl.ds(tok, 1), :], xbuf.at[buf_slot, pl.ds(j, 1), :],
                                     sem.at[buf_slot])

    def issue(tab, fresh, buf_slot):
        widx0 = jnp.where(fresh, 0, st[0])
        cur0 = jnp.where(fresh, tab[0, 0, 0], st[1])

        def body(j, carry):
            widx, cur = lax.while_loop(lambda c: c[1] == 0,
                                       lambda c: (c[0] + 1, tab[0, 0, c[0] + 1]), carry)
            tok = widx * HALF_BITS + _lowest_bit(cur, lut_ref)
            row_copy(tok, buf_slot, j).start()
            return widx, cur & (cur - 1)

        widx, cur = lax.fori_loop(0, ts, body, (widx0, cur0))
        st[0] = widx
        st[1] = cur

    @pl.when(s == 0)
    def _():
        issue(tab_ref, True, 0)

    last = jnp.logical_and(e == n_e - 1, k == kt - 1)

    @pl.when(jnp.logical_and(jnp.logical_not(last), k < kt - 1))
    def _():
        issue(tab_ref, False, 1 - slot)

    @pl.when(jnp.logical_and(jnp.logical_not(last), k == kt - 1))
    def _():
        issue(tabn_ref, True, 1 - slot)

    pltpu.make_async_copy(h_hbm.at[pl.ds(0, ts), :], xbuf.at[slot], sem.at[slot]).wait()
    x = xbuf[slot].astype(BF16)
    g = jnp.dot(x, wg_ref[0], preferred_element_type=F32)
    u = jnp.dot(x, wu_ref[0], preferred_element_type=F32)
    a = ((g * _sigmoid(g)) * u).astype(BF16)
    o_ref[...] = jnp.dot(a, wd_ref[0], preferred_element_type=F32)


def _ffn(h2, words_e, wg, wu, wd, cap, ts):
    n_tok, D = h2.shape
    E = N_EXPERTS
    kt = cap // ts
    nw = words_e.shape[-1]
    F = D_EXPERT
    return pl.pallas_call(
        functools.partial(_ffn_body, ts, kt),
        out_shape=jax.ShapeDtypeStruct((E * cap, D), F32),
        grid=(E, kt),
        in_specs=[
            pl.BlockSpec((1, 1, nw), lambda e, k: (e, 0, 0), memory_space=pltpu.SMEM),
            pl.BlockSpec((1, 1, nw), lambda e, k: (jnp.minimum(e + 1, E - 1), 0, 0), memory_space=pltpu.SMEM),
            pl.BlockSpec(memory_space=pltpu.SMEM),
            pl.BlockSpec(memory_space=pl.ANY),
            pl.BlockSpec((1, D, F), lambda e, k: (e, 0, 0)),
            pl.BlockSpec((1, D, F), lambda e, k: (e, 0, 0)),
            pl.BlockSpec((1, F, D), lambda e, k: (e, 0, 0)),
        ],
        out_specs=pl.BlockSpec((ts, D), lambda e, k: (e * kt + k, 0)),
        scratch_shapes=[pltpu.VMEM((2, ts, D), F32), pltpu.SemaphoreType.DMA((2,)), pltpu.SMEM((2,), I32)],
        compiler_params=_cparams(2), name="ffn",
    )(words_e, words_e, _lowbit_lut(), h2, wg, wu, wd)


def _combine_body(cap, words_ref, starts_ref, lut_ref, x_ref, mod_ref, aff_ref, ye_hbm, o_ref, buf, sem):
    E = N_EXPERTS
    tc = x_ref.shape[1]
    buf[...] = jnp.zeros(buf.shape, F32)

    def row_copy(src, e, t_local):
        return pltpu.make_async_copy(ye_hbm.at[pl.ds(src, 1), :], buf.at[e, pl.ds(t_local, 1), :], sem.at[0])

    def per_word(idx, carry):
        slot, total = carry
        e = idx // WORDS_PER_TILE
        w = idx % WORDS_PER_TILE
        slot = jnp.where(w == 0, starts_ref[0, 0, e] + e * cap, slot)

        def cond(c):
            return c[0] != 0

        def body(c):
            cur, slot, total = c
            t_local = w * HALF_BITS + _lowest_bit(cur, lut_ref)
            row_copy(slot, e, t_local).start()
            return cur & (cur - 1), slot + 1, total + 1

        _, slot, total = lax.while_loop(cond, body, (words_ref[0, 0, idx], slot, total))
        return slot, total

    _, total = lax.fori_loop(0, E * WORDS_PER_TILE, per_word, (jnp.int32(0), jnp.int32(0)))

    def wait_one(_, c):
        row_copy(0, 0, 0).wait()
        return c
    lax.fori_loop(0, total, wait_one, 0)

    aff = aff_ref[0]
    acc = jnp.zeros((tc, D_MODEL), F32)
    for e in range(E):
        acc = acc + aff[:, e:e + 1] * buf[e]
    m = mod_ref[0]
    o_ref[0] = x_ref[0] + m[:, 5 * D_MODEL:6 * D_MODEL] * acc


def _combine(x1, mod, aff_t, ye, words_t, starts_t, cap):
    B, n, D = x1.shape
    tc = LANES_V7X
    nt = n // tc
    mod_b = mod.shape[0]
    E = N_EXPERTS
    return pl.pallas_call(
        functools.partial(_combine_body, cap),
        out_shape=jax.ShapeDtypeStruct((B, n, D), F32),
        grid=(B, nt),
        in_specs=[
            pl.BlockSpec((1, 1, E * WORDS_PER_TILE), lambda b, i: (b * nt + i, 0, 0), memory_space=pltpu.SMEM),
            pl.BlockSpec((1, 1, E), lambda b, i: (b * nt + i, 0, 0), memory_space=pltpu.SMEM),
            pl.BlockSpec(memory_space=pltpu.SMEM),
            pl.BlockSpec((1, tc, D), lambda b, i: (b, i, 0)),
            pl.BlockSpec((1, 1, 6 * D), (lambda b, i: (b, 0, 0)) if mod_b > 1 else (lambda b, i: (0, 0, 0))),
            pl.BlockSpec((1, tc, E), lambda b, i: (b, i, 0)),
            pl.BlockSpec(memory_space=pl.ANY),
        ],
        out_specs=pl.BlockSpec((1, tc, D), lambda b, i: (b, i, 0)),
        scratch_shapes=[pltpu.VMEM((E, tc, D), F32), pltpu.SemaphoreType.DMA((1,))],
        compiler_params=_cparams(2), name="combine",
    )(words_t, starts_t, _lowbit_lut(), x1, mod, aff_t, ye)


def _expert_choice_ffn(x1, h2, aff3, aff_t, mod, P, l):
    B, n, D = x1.shape
    n_tok = B * n
    cap = EC_FACTOR * n_tok // N_EXPERTS
    nt = n_tok // LANES_V7X
    words3, starts3 = _route(aff3.reshape(nt, N_EXPERTS, LANES_V7X), cap)
    words = words3[:, :, :WORDS_PER_TILE]
    words_e = jnp.transpose(words, (1, 0, 2)).reshape(N_EXPERTS, 1, nt * WORDS_PER_TILE)
    words_t = words.reshape(nt, 1, N_EXPERTS * WORDS_PER_TILE)
    starts_t = starts3[:, :, 0].reshape(nt, 1, N_EXPERTS)
    ts = min(512, cap)
    ye = _ffn(h2.reshape(n_tok, D), words_e, P["wgate"][l], P["wup"][l], P["wdown"][l], cap, ts)
    return _combine(x1, mod, aff_t, ye, words_t, starts_t, cap)


def _layer(x, mod, P, l, ctx, rope_tabs):
    B, n, _ = x.shape
    latent = ctx is not None
    t_tile = 256
    outs = _proj(x, mod, P, l, rope_tabs if latent else None, not latent, t_tile)
    qa, ka, va, qb, kb, vb, qc, kc, vc, qd, kd, vd = outs[:12]
    if latent:
        c_ak, c_av, c_bk, c_bv, c_kc, c_vc, c_dk, c_dv = ctx
        tq, tk = 512, 512
    else:
        c_ak = c_av = c_bk = c_bv = c_kc = c_vc = c_dk = c_dv = None
        tq, tk = n, n
    lam_init = _lambda_init(l)
    oa = _attention(qa, ka, va, c_ak, c_av, scale=SCALE_A, kmap=lambda g: g, vmap=lambda g: g, tq=tq, tk=tk,
                    diff=lam_init, lam=P["alam"][l], subln=P["asubln"][l], name="attn_a")
    ob = _attention(qb, kb, vb, c_bk, c_bv, scale=SCALE_B, kmap=lambda g: g // 2, vmap=lambda g: g // 2,
                    tq=tq, tk=tk, name="attn_b")
    oc = _attention(qc, kc, vc, c_kc, c_vc, scale=SCALE_C, kmap=lambda g: g, vmap=lambda g: g, tq=tq, tk=tk,
                    name="attn_c")
    od = _attention(qd, kd, vd, c_dk, c_dv, scale=SCALE_D, kmap=lambda g: g // 2, vmap=lambda g: g // 2,
                    tq=tq, tk=tk, window=WINDOW if latent else None, sink=P["dsink"][l], name="attn_d")
    x1, h2, aff3, aff_t = _merge(x, mod, P, l, (oa, ob, oc, od), t_tile)
    x2 = _expert_choice_ffn(x1, h2, aff3, aff_t, mod, P, l)
    return x2, outs[12:]


def _rope_tables(n, dim):
    rows = n // GRID_W
    row = jnp.repeat(jnp.arange(rows, dtype=F32), GRID_W)
    col = jnp.tile(jnp.arange(GRID_W, dtype=F32), rows)
    nf = dim // 4
    inv = ROPE_THETA ** (-jnp.arange(nf, dtype=F32) / nf)
    ar = row[:, None] * inv
    ac = col[:, None] * inv
    ang = jnp.concatenate([ar, ar, ac, ac], axis=-1)
    cos, sin = jnp.cos(ang), jnp.sin(ang)
    first = (jnp.arange(dim) % (2 * nf)) < nf
    sin_a = jnp.where(first[None, :], -sin, 0.0)
    sin_b = jnp.where(first[None, :], 0.0, sin)
    reps = LANES_V7X // dim
    return tuple(jnp.tile(t, (1, reps)) for t in (cos, sin_a, sin_b))


def _prepare_params(ada_w, ada_b, norm1_g, norm2_g, w_in, a_q_norm, a_k_norm, a_lambda, a_subln, b_q_norm,
                    b_k_norm, c_qa_norm, c_kva_norm, c_wq_up, c_wkv_up, c_q_norm, c_k_norm, c_qr_norm,
                    c_kr_norm, d_q_norm, d_k_norm, d_sink, w_branch, w_out, w_router, w_gate, w_up, w_down):
    L = w_in.shape[0]

    def row(a):
        return a[:, None, :].astype(F32)

    def tiled(a, reps):
        return jnp.tile(a, (1, reps))[:, None, :].astype(F32)

    w1 = jnp.concatenate([w_in[:, :, :_CKR_END_ORIG],
                          jnp.zeros((L, D_MODEL, _C_CKR[1] - _C_CKR[0] - ROPE_C), w_in.dtype),
                          w_in[:, :, _CKR_END_ORIG:_W_ORIG_NOGATE]], axis=-1).astype(BF16)
    wq = c_wq_up.reshape(L, Q_LORA, H_C, NOPE_C + ROPE_C)
    wqup = jnp.concatenate([wq[..., :NOPE_C].reshape(L, Q_LORA, H_C * NOPE_C),
                            wq[..., NOPE_C:].reshape(L, Q_LORA, H_C * ROPE_C)], axis=-1).astype(BF16)
    wkv = c_wkv_up.reshape(L, KV_LORA, H_C, NOPE_C + V_C)
    wkvup = jnp.concatenate([wkv[..., :NOPE_C].reshape(L, KV_LORA, H_C * NOPE_C),
                             wkv[..., NOPE_C:].reshape(L, KV_LORA, H_C * V_C)], axis=-1).astype(BF16)
    ckrn = jnp.concatenate([c_kr_norm, jnp.zeros((L, LANES_V7X - ROPE_C), c_kr_norm.dtype)], axis=-1)
    return dict(
        g1=row(norm1_g), g2=row(norm2_g), w1=w1,
        aqn=tiled(a_q_norm, 2 * H_A), akn=tiled(a_k_norm, 2 * H_A),
        bqn=tiled(b_q_norm, H_B), bkn=tiled(b_k_norm, KV_B),
        cqan=row(c_qa_norm), ckvan=row(c_kva_norm), wqup=wqup, wkvup=wkvup,
        cqn=tiled(c_q_norm, H_C), ckn=tiled(c_k_norm, H_C), cqrn=tiled(c_qr_norm, H_C), ckrn=row(ckrn),
        dqn=tiled(d_q_norm, H_D), dkn=tiled(d_k_norm, KV_D),
        alam=a_lambda.astype(F32), asubln=row(a_subln), dsink=d_sink.astype(F32),
        wgates=w_in[:, :, _W_ORIG_NOGATE:].astype(BF16), wbranch=w_branch.astype(BF16), wout=w_out.astype(BF16),
        wrouter=w_router.astype(BF16), wrouter_t=jnp.transpose(w_router, (0, 2, 1)).astype(BF16),
        wgate=w_gate.astype(BF16), wup=w_up.astype(BF16), wdown=w_down.astype(BF16),
    )


def _head_major(c):
    return jnp.transpose(c, (0, 2, 1, 3)).astype(BF16)


def kernel(x_prompt, x_sample, c, cache_a_k, cache_a_v, cache_b_k, cache_b_v, cache_c_kv, cache_c_kr, cache_d_k, cache_d_v, c_ctx, ada_w, ada_b, norm1_g, norm2_g, w_in, a_q_norm, a_k_norm, a_lambda, a_subln, b_q_norm, b_k_norm, c_qa_norm, c_kva_norm, c_wq_up, c_wkv_up, c_q_norm, c_k_norm, c_qr_norm, c_kr_norm, d_q_norm, d_k_norm, d_sink, w_branch, w_out, w_router, w_gate, w_up, w_down):
    L = w_in.shape[0]
    P = _prepare_params(ada_w, ada_b, norm1_g, norm2_g, w_in, a_q_norm, a_k_norm, a_lambda, a_subln, b_q_norm,
                        b_k_norm, c_qa_norm, c_kva_norm, c_wq_up, c_wkv_up, c_q_norm, c_k_norm, c_qr_norm,
                        c_kr_norm, d_q_norm, d_k_norm, d_sink, w_branch, w_out, w_router, w_gate, w_up, w_down)
    dec_b = c.shape[0]
    rows = 8 * ((1 + dec_b + 7) // 8)
    cond = jnp.concatenate([c_ctx[None, :], c, jnp.zeros((rows - 1 - dec_b, D_MODEL), F32)], axis=0)
    mod = _modulation(cond, ada_w, ada_b)
    mod_ctx = mod[:, 0:1, None, :]
    mod_lat = mod[:, 1:1 + dec_b, None, :]

    x = x_prompt
    caches = []
    for l in range(L):
        x, cache = _layer(x, mod_ctx[l], P, l, None, None)
        caches.append(cache)
    y_prompt = x
    B, n = x_prompt.shape[:2]
    new = [jnp.stack([lc[j] for lc in caches], axis=1) for j in range(8)]
    new_a_k = new[0].reshape(B, L, n, 2 * H_A, HD_A)
    new_a_v = new[1].reshape(B, L, n, H_A, 2 * HD_A)
    new_b_k = new[2].reshape(B, L, n, KV_B, HD_B)
    new_b_v = new[3].reshape(B, L, n, KV_B, HD_B)
    new_c_kv, new_c_kr = new[4], new[5]
    new_d_k = new[6].reshape(B, L, n, KV_D, HD_D)
    new_d_v = new[7].reshape(B, L, n, KV_D, HD_D)

    n_lat = x_sample.shape[1]
    rope_tabs = _rope_tables(n_lat, HD_A) + _rope_tables(n_lat, ROPE_C)
    x = x_sample
    for l in range(L):
        kc_ctx, vc_ctx = _mla_cache(cache_c_kv[:, l], cache_c_kr[:, l], P["wkvup"][l], P["ckn"][l])
        ctx = (_head_major(cache_a_k[:, l]), _head_major(cache_a_v[:, l]),
               _head_major(cache_b_k[:, l]), _head_major(cache_b_v[:, l]),
               kc_ctx, vc_ctx,
               _head_major(cache_d_k[:, l]), _head_major(cache_d_v[:, l]))
        x, _ = _layer(x, mod_lat[l], P, l, ctx, rope_tabs)
    y_sample = x
    return (y_prompt, y_sample, new_a_k, new_a_v, new_b_k, new_b_v, new_c_kv, new_c_kr, new_d_k, new_d_v)
```

```python
import functools
import math

import numpy as np
import jax
import jax.numpy as jnp
from jax import lax
from jax.experimental import pallas as pl
from jax.experimental.pallas import tpu as pltpu

F32 = jnp.float32
BF16 = jnp.bfloat16
I32 = jnp.int32

D_MODEL = 1024
GRID_W = 64
ROPE_THETA = 10000.0
EPS = 1e-6
NEG_BIG = -1e30
LOG2E = 1.4426950408889634

H_A, HD_A = 4, 64
H_B, KV_B, HD_B = 8, 2, 64
H_C, Q_LORA, KV_LORA, NOPE_C, ROPE_C, V_C = 8, 384, 256, 64, 32, 64
H_D, KV_D, HD_D = 8, 2, 64
WINDOW = 128
N_BRANCH, BRANCH_W = 4, 512
N_EXPERTS, D_EXPERT, EC_FACTOR = 16, 1408, 2

SCALE_A = HD_A ** -0.5
SCALE_B = HD_B ** -0.5
SCALE_C = (NOPE_C + ROPE_C) ** -0.5
SCALE_D = HD_D ** -0.5

LANES_V7X = 128
VMEM_LIMIT_BYTES_V7X = 56 * 1024 * 1024

_W_ORIG_NOGATE = 3744
_C_AQ, _C_AK, _C_AV = (0, 512), (512, 1024), (1024, 1536)
_C_BQ, _C_BK, _C_BV = (1536, 2048), (2048, 2176), (2176, 2304)
_C_CQ, _C_CKV, _C_CKR = (2304, 2688), (2688, 2944), (2944, 3072)
_C_DQ, _C_DK, _C_DV = (3072, 3584), (3584, 3712), (3712, 3840)
_W1_COLS = 3840
_CKR_END_ORIG = 2976

NO_SLOT = -(2 ** 30)


def _lambda_init(l):
    return 0.8 - 0.6 * math.exp(-0.3 * l)


def _cparams(n_axes):
    return pltpu.CompilerParams(
        dimension_semantics=("arbitrary",) * n_axes,
        vmem_limit_bytes=VMEM_LIMIT_BYTES_V7X,
    )


def _const_spec(shape):
    nd = len(shape)
    return pl.BlockSpec(shape, lambda *_: (0,) * nd)


def _inv_rms(z):
    return lax.rsqrt(jnp.mean(z * z, axis=-1, keepdims=True) + EPS)


def _sigmoid(x):
    return 1.0 / (1.0 + jnp.exp(-x))


def _mod_body(c_ref, w_ref, b_ref, o_ref):
    c = c_ref[...]
    s = (c * _sigmoid(c)).astype(BF16)
    o_ref[0] = jnp.dot(s, w_ref[0].astype(BF16), preferred_element_type=F32) + b_ref[0]


def _modulation(cond, ada_w, ada_b):
    L = ada_w.shape[0]
    R = cond.shape[0]
    nblk = 6
    return pl.pallas_call(
        _mod_body,
        out_shape=jax.ShapeDtypeStruct((L, R, 6 * D_MODEL), F32),
        grid=(L, nblk),
        in_specs=[
            pl.BlockSpec((R, D_MODEL), lambda l, j: (0, 0)),
            pl.BlockSpec((1, D_MODEL, D_MODEL), lambda l, j: (l, 0, j)),
            pl.BlockSpec((1, 1, D_MODEL), lambda l, j: (l, 0, j)),
        ],
        out_specs=pl.BlockSpec((1, R, D_MODEL), lambda l, j: (l, 0, j)),
        compiler_params=_cparams(2),
        name="modulation",
    )(cond, ada_w, ada_b.reshape(L, 1, 6 * D_MODEL))


def _ada_norm(x, g, shift, scale):
    return (x * _inv_rms(x) * g) * (1.0 + scale) + shift


def _rotary(zg, cos, sin_a, sin_b, nf):
    w = zg.shape[-1]
    return zg * cos + pltpu.roll(zg, w - nf, 1) * sin_a + pltpu.roll(zg, nf, 1) * sin_b


def _tile_lanes(t, width):
    reps = width // t.shape[-1]
    return t if reps == 1 else jnp.concatenate([t] * reps, axis=1)


def _head_norm(z, g_tiled, n_heads, hd, rope):
    zg = z * g_tiled
    if rope is not None:
        cos, sin_a, sin_b, nf = rope
        w = z.shape[-1]
        zg = _rotary(zg, _tile_lanes(cos, w), _tile_lanes(sin_a, w), _tile_lanes(sin_b, w), nf)
    outs = []
    for h in range(n_heads):
        zh = z[:, h * hd:(h + 1) * hd]
        outs.append(zg[:, h * hd:(h + 1) * hd] * _inv_rms(zh))
    return outs


def _mla_expand(ckv_n, kr, wkv_ref, ckn_tiled, kc_ref, vc_ref):
    kv = jnp.dot(ckv_n.astype(BF16), wkv_ref[...], preferred_element_type=F32)
    kn = _head_norm(kv[:, :H_C * NOPE_C], ckn_tiled, H_C, NOPE_C, None)
    krb = kr.astype(BF16)
    t = kv.shape[0]
    pad = jnp.zeros((t, LANES_V7X - NOPE_C - ROPE_C), BF16)
    for h in range(H_C):
        kc_ref[0, h, :, 0:NOPE_C] = kn[h].astype(BF16)
        kc_ref[0, h, :, NOPE_C:NOPE_C + ROPE_C] = krb
        kc_ref[0, h, :, NOPE_C + ROPE_C:LANES_V7X] = pad
    vt = kv[:, H_C * NOPE_C:].astype(BF16).T
    for h in range(H_C):
        vc_ref[0, h] = vt[h * V_C:(h + 1) * V_C, :]


def _proj_body(rope, emit_cache, *refs):
    it = iter(refs)
    x_ref, mod_ref, g1_ref, w1_ref = next(it), next(it), next(it), next(it)
    aqn, akn, bqn, bkn = next(it), next(it), next(it), next(it)
    cqan, ckvan, wqup_ref, wkvup_ref = next(it), next(it), next(it), next(it)
    cqn, ckn, cqrn, ckrn, dqn, dkn = next(it), next(it), next(it), next(it), next(it), next(it)
    if rope:
        cos64, sa64, sb64, cos32, sa32, sb32 = (next(it)[...] for _ in range(6))
        rope64 = (cos64, sa64, sb64, HD_A // 4)
        rope32 = (cos32, sa32, sb32, ROPE_C // 4)
    else:
        rope64 = rope32 = None
    qa_ref, ka_ref, va_ref = next(it), next(it), next(it)
    qb_ref, kb_ref, vb_ref = next(it), next(it), next(it)
    qc_ref, kc_ref, vc_ref = next(it), next(it), next(it)
    qd_ref, kd_ref, vd_ref = next(it), next(it), next(it)
    if emit_cache:
        c_ak, c_av, c_bk, c_bv, c_ckv, c_ckr, c_dk, c_dv = (next(it) for _ in range(8))

    x = x_ref[0]
    m = mod_ref[0]
    hb = _ada_norm(x, g1_ref[...], m[:, 0:D_MODEL], m[:, D_MODEL:2 * D_MODEL]).astype(BF16)
    t = x.shape[0]

    def seg(c):
        return jnp.dot(hb, w1_ref[:, c[0]:c[1]], preferred_element_type=F32)

    def store_heads(ys, ref):
        for h, y in enumerate(ys):
            ref[0, h] = y.astype(BF16)

    def store_cache(ys, ref, hd):
        for h, y in enumerate(ys):
            ref[0, :, h * hd:(h + 1) * hd] = y

    store_heads(_head_norm(seg(_C_AQ), aqn[...], 2 * H_A, HD_A, rope64), qa_ref)
    ys = _head_norm(seg(_C_AK), akn[...], 2 * H_A, HD_A, rope64)
    store_heads(ys, ka_ref)
    if emit_cache:
        store_cache(ys, c_ak, HD_A)
    av = seg(_C_AV)
    avt = av.astype(BF16).T
    for h in range(H_A):
        va_ref[0, h] = avt[h * 2 * HD_A:(h + 1) * 2 * HD_A, :]
    if emit_cache:
        c_av[0] = av

    for (cq, ck, cv, qn, kn_, q_ref, k_ref, v_ref, ck_ref, cv_ref) in (
        (_C_BQ, _C_BK, _C_BV, bqn, bkn, qb_ref, kb_ref, vb_ref,
         c_bk if emit_cache else None, c_bv if emit_cache else None),
        (_C_DQ, _C_DK, _C_DV, dqn, dkn, qd_ref, kd_ref, vd_ref,
         c_dk if emit_cache else None, c_dv if emit_cache else None),
    ):
        store_heads(_head_norm(seg(cq), qn[...], H_B, HD_B, rope64), q_ref)
        ys = _head_norm(seg(ck), kn_[...], KV_B, HD_B, rope64)
        store_heads(ys, k_ref)
        v = seg(cv)
        if v_ref is vb_ref:
            vt = v.astype(BF16).T
            for h in range(KV_B):
                v_ref[0, h] = vt[h * HD_B:(h + 1) * HD_B, :]
        else:
            for h in range(KV_B):
                v_ref[0, h] = v[:, h * HD_B:(h + 1) * HD_B].astype(BF16)
        if emit_cache:
            store_cache(ys, ck_ref, HD_B)
            cv_ref[0] = v

    cq = seg(_C_CQ)
    cq_n = (cq * _inv_rms(cq) * cqan[...]).astype(BF16)
    zq = jnp.dot(cq_n, wqup_ref[...], preferred_element_type=F32)
    q_nope = _head_norm(zq[:, :H_C * NOPE_C], cqn[...], H_C, NOPE_C, None)
    q_rope = _head_norm(zq[:, H_C * NOPE_C:], cqrn[...], H_C, ROPE_C, rope32)
    pad = jnp.zeros((t, LANES_V7X - NOPE_C - ROPE_C), BF16)
    for h in range(H_C):
        qc_ref[0, h, :, 0:NOPE_C] = q_nope[h].astype(BF16)
        qc_ref[0, h, :, NOPE_C:NOPE_C + ROPE_C] = q_rope[h].astype(BF16)
        qc_ref[0, h, :, NOPE_C + ROPE_C:LANES_V7X] = pad

    ckv = seg(_C_CKV)
    ckv_n = ckv * _inv_rms(ckv) * ckvan[...]
    ckr = seg(_C_CKR)
    kr_plain = ckr * ckrn[...] * _inv_rms(ckr[:, :ROPE_C])
    if emit_cache:
        c_ckv[0] = ckv_n
        c_ckr[0] = kr_plain[:, :ROPE_C]
    if rope:
        zg = ckr * ckrn[...]
        kr = _rotary(zg, cos32, sa32, sb32, ROPE_C // 4) * _inv_rms(ckr[:, :ROPE_C])
    else:
        kr = kr_plain
    _mla_expand(ckv_n, kr[:, :ROPE_C], wkvup_ref, ckn[...], kc_ref, vc_ref)


def _proj(x, mod, P, l, rope_tabs, emit_cache, t_tile):
    B, n, _ = x.shape
    nt = n // t_tile
    mod_b = mod.shape[0]
    grid = (B, nt)

    def tok_spec(w):
        return pl.BlockSpec((1, t_tile, w), lambda b, i: (b, i, 0))

    def head_spec(h, d):
        return pl.BlockSpec((1, h, t_tile, d), lambda b, i: (b, 0, i, 0))

    small = [P["aqn"][l], P["akn"][l], P["bqn"][l], P["bkn"][l], P["cqan"][l], P["ckvan"][l],
             P["wqup"][l], P["wkvup"][l], P["cqn"][l], P["ckn"][l], P["cqrn"][l], P["ckrn"][l],
             P["dqn"][l], P["dkn"][l]]
    ins = [x, mod, P["g1"][l], P["w1"][l]] + small
    in_specs = [
        tok_spec(D_MODEL),
        pl.BlockSpec((1, 1, 6 * D_MODEL), (lambda b, i: (b, 0, 0)) if mod_b > 1 else (lambda b, i: (0, 0, 0))),
        _const_spec((1, D_MODEL)),
        pl.BlockSpec((D_MODEL, _W1_COLS), lambda b, i: (0, 0), pipeline_mode=pl.Buffered(1)),
    ] + [_const_spec(a.shape) for a in small]
    rope = rope_tabs is not None
    if rope:
        ins += list(rope_tabs)
        in_specs += [pl.BlockSpec((t_tile, LANES_V7X), lambda b, i: (i, 0)) for _ in rope_tabs]

    def hs(h, d):
        return jax.ShapeDtypeStruct((B, h, n, d), BF16)

    def hst(h, d):
        return jax.ShapeDtypeStruct((B, h, d, n), BF16)

    def head_spec_t(h, d):
        return pl.BlockSpec((1, h, d, t_tile), lambda b, i: (b, 0, 0, i))

    out_shape = [hs(8, 64), hs(8, 64), hst(4, 128), hs(8, 64), hs(2, 64), hst(2, 64),
                 hs(8, 128), hs(8, 128), hst(8, 64), hs(8, 64), hs(2, 64), hs(2, 64)]
    out_specs = [head_spec(8, 64), head_spec(8, 64), head_spec_t(4, 128), head_spec(8, 64), head_spec(2, 64),
                 head_spec_t(2, 64), head_spec(8, 128), head_spec(8, 128), head_spec_t(8, 64), head_spec(8, 64),
                 head_spec(2, 64), head_spec(2, 64)]
    if emit_cache:
        for w in (512, 512, 128, 128, KV_LORA, ROPE_C, 128, 128):
            out_shape.append(jax.ShapeDtypeStruct((B, n, w), F32))
            out_specs.append(tok_spec(w))
    return pl.pallas_call(
        functools.partial(_proj_body, rope, emit_cache),
        out_shape=out_shape, grid=grid, in_specs=in_specs, out_specs=out_specs,
        compiler_params=_cparams(2), name="proj",
    )(*ins)


def _mla_cache_body(ckv_ref, ckr_ref, wkvup_ref, ckn_ref, kc_ref, vc_ref):
    _mla_expand(ckv_ref[0], ckr_ref[0], wkvup_ref, ckn_ref[...], kc_ref, vc_ref)


def _mla_cache(ckv, ckr, wkvup, ckn_tiled):
    B, K, _ = ckv.shape
    return pl.pallas_call(
        _mla_cache_body,
        out_shape=[jax.ShapeDtypeStruct((B, H_C, K, LANES_V7X), BF16),
                   jax.ShapeDtypeStruct((B, H_C, V_C, K), BF16)],
        grid=(B,),
        in_specs=[pl.BlockSpec((1, K, KV_LORA), lambda b: (b, 0, 0)),
                  pl.BlockSpec((1, K, ROPE_C), lambda b: (b, 0, 0)),
                  _const_spec(wkvup.shape), _const_spec(ckn_tiled.shape)],
        out_specs=[pl.BlockSpec((1, H_C, K, LANES_V7X), lambda b: (b, 0, 0, 0)),
                   pl.BlockSpec((1, H_C, V_C, K), lambda b: (b, 0, 0, 0))],
        compiler_params=_cparams(1), name="mla_cache",
    )(ckv, ckr, wkvup, ckn_tiled)


def _attn_body(cfg, *refs):
    it = iter(refs)
    q_ref = next(it)
    kc_ref = vc_ref = None
    if cfg["has_ctx"]:
        kc_ref, vc_ref = next(it), next(it)
    ko_ref, vo_ref = next(it), next(it)
    sink_ref = next(it) if cfg["sink"] else None
    if cfg["diff"] is not None:
        lam_ref, subln_ref = next(it), next(it)
    o_ref = next(it)

    R, RK, RV = cfg["R"], cfg["RK"], cfg["RV"]
    tq, tk, ko = cfg["tq"], cfg["tk"], cfg["ko"]
    c = cfg["scale"] * LOG2E
    dv = vo_ref.shape[-2] if cfg["transposed"] else vo_ref.shape[-1]
    g = pl.program_id(1)
    i = pl.program_id(2)
    nt_dims = (((1,), (1,)), ((), ()))
    rks = [r if RK > 1 else 0 for r in range(R)]
    rvs = [r if RV > 1 else 0 for r in range(R)]
    qs = [(q_ref[0, r].astype(F32) * c).astype(BF16) for r in range(R)]

    tr = cfg["transposed"]
    red = 0 if tr else -1

    def scores(q, k):
        if tr:
            return lax.dot_general(k, q, nt_dims, preferred_element_type=F32)
        return lax.dot_general(q, k, nt_dims, preferred_element_type=F32)

    def update(s, v, carry, mask):
        m, l, acc = carry
        if mask is not None:
            s = jnp.where(mask, s, NEG_BIG)
        m_new = jnp.maximum(m, jnp.max(s, axis=red, keepdims=True))
        alpha = jnp.exp2(m - m_new)
        p = jnp.exp2(s - m_new)
        l = alpha * l + jnp.sum(p, axis=red, keepdims=True)
        if tr:
            acc = alpha * acc + jnp.dot(v, p.astype(BF16), preferred_element_type=F32)
        else:
            acc = alpha * acc + jnp.dot(p.astype(BF16), v, preferred_element_type=F32)
        return m_new, l, acc

    def kv(k_ref, v_ref, r, st, w):
        if st is None:
            return k_ref[0, rks[r]], v_ref[0, rvs[r]]
        if tr:
            return k_ref[0, rks[r], pl.ds(st, w), :], v_ref[0, rvs[r], :, pl.ds(st, w)]
        return k_ref[0, rks[r], pl.ds(st, w), :], v_ref[0, rvs[r], pl.ds(st, w), :]

    def run_chunks(chunks, carries):
        def issue(ch):
            k_ref, v_ref, st, w, mask = ch
            kvs = [kv(k_ref, v_ref, r, st, w) for r in range(R)]
            return [scores(qs[r], kvs[r][0]) for r in range(R)], [x[1] for x in kvs], mask
        nxt = issue(chunks[0])
        for ci in range(len(chunks)):
            ss, vs, mask = nxt
            if ci + 1 < len(chunks):
                nxt = issue(chunks[ci + 1])
            carries = tuple(update(ss[r], vs[r], carries[r], mask) for r in range(R))
        return carries

    stat = (1, tq) if tr else (tq, 1)
    carries = []
    for r in range(R):
        if cfg["sink"]:
            m0 = jnp.full(stat, sink_ref[g * R + r] * LOG2E, F32)
            l0 = jnp.ones(stat, F32)
        else:
            m0 = jnp.full(stat, NEG_BIG, F32)
            l0 = jnp.zeros(stat, F32)
        carries.append((m0, l0, jnp.zeros((dv, tq) if tr else (tq, dv), F32)))
    chunks = []
    if cfg["has_ctx"]:
        chunks.append((kc_ref, vc_ref, None, None, None))
    if cfg["window"] is not None:
        win = cfg["window"]
        wk = min(tq + 2 * win, ko)
        q0 = i * tq
        k0 = pl.multiple_of(jnp.clip(q0 - win, 0, ko - wk), win)
        qpos = q0 + lax.broadcasted_iota(I32, (tq, wk), 0)
        kpos = k0 + lax.broadcasted_iota(I32, (tq, wk), 1)
        chunks.append((ko_ref, vo_ref, k0, wk, jnp.abs(qpos - kpos) <= win))
    elif ko == tk:
        chunks.append((ko_ref, vo_ref, None, None, None))
    else:
        for j in range(ko // tk):
            chunks.append((ko_ref, vo_ref, j * tk, tk, None))
    carries = run_chunks(chunks, tuple(carries))
    outs = [acc / l for (_, l, acc) in carries]

    if cfg["diff"] is not None:
        lp = lam_ref[...]
        lam = (jnp.exp(jnp.sum(lp[0:1] * lp[1:2], axis=-1, keepdims=True))
               - jnp.exp(jnp.sum(lp[2:3] * lp[3:4], axis=-1, keepdims=True)) + cfg["diff"])
        d = outs[0] - lam * outs[1]
        if tr:
            inv = lax.rsqrt(jnp.mean(d * d, axis=0, keepdims=True) + EPS)
            o_ref[0] = (d * inv * subln_ref[...] * (1.0 - cfg["diff"])).T.astype(BF16)
        else:
            o_ref[0] = (d * _inv_rms(d) * subln_ref[...] * (1.0 - cfg["diff"])).astype(BF16)
    elif tr:
        o_ref[0] = jnp.concatenate(outs, axis=0).T.astype(BF16)
    else:
        for r in range(R):
            o_ref[0, :, r * dv:(r + 1) * dv] = outs[r].astype(BF16)


def _attention(q, ko, vo, kc, vc, *, scale, kmap, vmap, tq, tk, window=None, sink=None, diff=None,
               lam=None, subln=None, transposed=True, name="attn"):
    B, hq, n, dk = q.shape
    R = 2
    k_heads = ko.shape[1]
    v_heads = vo.shape[1]
    RK = R if k_heads == hq else 1
    RV = R if v_heads == hq else 1
    k_own = ko.shape[2]
    has_ctx = kc is not None
    cfg = dict(R=R, RK=RK, RV=RV, tq=tq, tk=min(tk, k_own), ko=k_own, scale=scale, has_ctx=has_ctx,
               window=window, sink=sink is not None, diff=diff, transposed=transposed)
    grid = (B, hq // R, n // tq)
    ins = [q]
    in_specs = [pl.BlockSpec((1, R, tq, dk), lambda b, g, i: (b, g, i, 0))]

    def kv_spec(arr, r, hmap):
        return pl.BlockSpec((1, r, arr.shape[2], arr.shape[3]), lambda b, g, i: (b, hmap(g), 0, 0))

    if has_ctx:
        ins += [kc, vc]
        in_specs += [kv_spec(kc, RK, kmap), kv_spec(vc, RV, vmap)]
    ins += [ko, vo]
    in_specs += [kv_spec(ko, RK, kmap), kv_spec(vo, RV, vmap)]
    if sink is not None:
        ins.append(sink)
        in_specs.append(pl.BlockSpec(memory_space=pltpu.SMEM))
    if diff is not None:
        ins += [lam, subln]
        in_specs += [_const_spec(lam.shape), _const_spec(subln.shape)]
    out_w = (hq // R) * LANES_V7X
    return pl.pallas_call(
        functools.partial(_attn_body, cfg),
        out_shape=jax.ShapeDtypeStruct((B, n, out_w), BF16),
        grid=grid, in_specs=in_specs,
        out_specs=pl.BlockSpec((1, tq, LANES_V7X), lambda b, g, i: (b, i, g)),
        compiler_params=_cparams(3), name=name,
    )(*ins)


def _merge_body(x_ref, mod_ref, g1_ref, g2_ref, oa_ref, ob_ref, oc_ref, od_ref, wg_ref, wb_ref, wo_ref,
                wrt_ref, x1_ref, h2_ref, aff3_ref):
    x = x_ref[0]
    m = mod_ref[0]
    D = D_MODEL
    hb = _ada_norm(x, g1_ref[...], m[:, 0:D], m[:, D:2 * D]).astype(BF16)
    merged = None
    for bi, o_ref in enumerate((oa_ref, ob_ref, oc_ref, od_ref)):
        gate = _sigmoid(jnp.dot(hb, wg_ref[:, bi * D:(bi + 1) * D], preferred_element_type=F32))
        br = jnp.dot(o_ref[0], wb_ref[bi], preferred_element_type=F32)
        merged = gate * br if merged is None else merged + gate * br
    mix = jnp.dot(merged.astype(BF16), wo_ref[...], preferred_element_type=F32)
    x1 = x + m[:, 2 * D:3 * D] * mix
    x1_ref[0] = x1
    h2 = _ada_norm(x1, g2_ref[...], m[:, 3 * D:4 * D], m[:, 4 * D:5 * D])
    h2b = h2.astype(BF16)
    h2_ref[0] = h2b
    lt = lax.dot_general(wrt_ref[...], h2b, (((1,), (1,)), ((), ())), preferred_element_type=F32)
    et = jnp.exp(lt - jnp.max(lt, axis=0, keepdims=True))
    at = et / jnp.sum(et, axis=0, keepdims=True)
    for ti in range(x.shape[0] // LANES_V7X):
        aff3_ref[0, ti] = at[:, ti * LANES_V7X:(ti + 1) * LANES_V7X]


def _merge(x, mod, P, l, outs, t_tile):
    B, n, D = x.shape
    nt = n // t_tile
    mod_b = mod.shape[0]
    tl = t_tile // LANES_V7X

    def tok_spec(w):
        return pl.BlockSpec((1, t_tile, w), lambda b, i: (b, i, 0))

    def w_spec(shape):
        nd = len(shape)
        return pl.BlockSpec(shape, lambda b, i: (0,) * nd, pipeline_mode=pl.Buffered(1))

    ins = [x, mod, P["g1"][l], P["g2"][l], *outs, P["wgates"][l], P["wbranch"][l], P["wout"][l],
           P["wrouter_t"][l]]
    in_specs = [
        tok_spec(D),
        pl.BlockSpec((1, 1, 6 * D), (lambda b, i: (b, 0, 0)) if mod_b > 1 else (lambda b, i: (0, 0, 0))),
        _const_spec((1, D)), _const_spec((1, D)),
        tok_spec(BRANCH_W), tok_spec(BRANCH_W), tok_spec(BRANCH_W), tok_spec(BRANCH_W),
        w_spec((D, N_BRANCH * D)), w_spec((N_BRANCH, BRANCH_W, D)), w_spec((D, D)),
        _const_spec((N_EXPERTS, D)),
    ]
    out_shape = [
        jax.ShapeDtypeStruct((B, n, D), F32),
        jax.ShapeDtypeStruct((B, n, D), BF16),
        jax.ShapeDtypeStruct((B, n // LANES_V7X, N_EXPERTS, LANES_V7X), F32),
    ]
    out_specs = [
        tok_spec(D), tok_spec(D),
        pl.BlockSpec((1, tl, N_EXPERTS, LANES_V7X), lambda b, i: (b, i, 0, 0)),
    ]
    return pl.pallas_call(
        _merge_body, out_shape=out_shape, grid=(B, nt), in_specs=in_specs, out_specs=out_specs,
        compiler_params=_cparams(2), name="merge",
    )(*ins)


def _route_body(cap, aff_ref, tri_ref, ones_ref, pos_ref, starts_ref, tot_sc, off_sc):
    nt = aff_ref.shape[0]
    E = N_EXPERTS
    bits = pltpu.bitcast(aff_ref[...], I32)

    def count_ge(th):
        c = jnp.sum(jnp.where(bits >= th[None], 1, 0), axis=0)
        return jnp.broadcast_to(jnp.sum(c, axis=1, keepdims=True), (E, LANES_V7X))

    def bisect(_, carry):
        lo, hi = carry
        mid = lo + ((hi - lo) >> 1)
        ok = count_ge(mid) >= cap
        return jnp.where(ok, mid, lo), jnp.where(ok, hi, mid)

    lo0 = jnp.zeros((E, LANES_V7X), I32)
    hi0 = jnp.full((E, LANES_V7X), 0x7F800000, I32)
    thr, _ = lax.fori_loop(0, 31, bisect, (lo0, hi0))

    gt = bits > thr[None]
    eq = bits == thr[None]
    gt_f = jnp.where(gt, 1.0, 0.0)
    eq_f = jnp.where(eq, 1.0, 0.0)
    n_gt = jnp.sum(jnp.sum(gt_f, axis=0), axis=1, keepdims=True)
    need = jnp.broadcast_to(float(cap) - n_gt, (E, LANES_V7X))[None]

    def prefix(mask_f):
        m2 = mask_f.astype(BF16).reshape(nt * E, LANES_V7X)
        incl = jnp.dot(m2, tri_ref[...], preferred_element_type=F32).reshape(nt, E, LANES_V7X)
        tot_sc[...] = jnp.dot(m2, ones_ref[...], preferred_element_type=F32).reshape(nt, E, LANES_V7X)

        def scan(i, run):
            off_sc[i] = run
            return run + tot_sc[i]
        lax.fori_loop(0, nt, scan, jnp.zeros((E, LANES_V7X), F32))
        offs = off_sc[...]
        return incl - mask_f + offs, offs

    ex_gt, off_gt = prefix(gt_f)
    ex_eq, off_eq = prefix(eq_f)
    sel = jnp.logical_or(gt, jnp.logical_and(eq, ex_eq < need))
    pos = ex_gt + jnp.minimum(ex_eq, need)
    pos_ref[...] = jnp.where(sel, pos.astype(I32), NO_SLOT)
    starts_ref[...] = (off_gt + jnp.minimum(off_eq, need)).astype(I32)


def _route(aff3, cap):
    nt = aff3.shape[0]
    j = np.arange(LANES_V7X)
    tri = jnp.asarray(j[:, None] <= j[None, :], BF16)
    ones = jnp.ones((LANES_V7X, LANES_V7X), BF16)
    shp = (nt, N_EXPERTS, LANES_V7X)
    return pl.pallas_call(
        functools.partial(_route_body, cap),
        out_shape=[jax.ShapeDtypeStruct(shp, I32), jax.ShapeDtypeStruct(shp, I32)],
        scratch_shapes=[pltpu.VMEM(shp, F32), pltpu.VMEM(shp, F32)],
        compiler_params=pltpu.CompilerParams(vmem_limit_bytes=VMEM_LIMIT_BYTES_V7X),
        name="route",
    )(aff3, tri, ones)


GATHER_CHUNK = 2 * LANES_V7X


def _ffn_body(ts, kt, nt, starts_ref, pos_ref, h_hbm, wg_ref, wu_ref, wd_ref, o_ref, xe_ref, hbuf, sem, cur):
    k = pl.program_id(1)
    slot0 = k * ts
    @pl.when(k == 0)
    def _():
        cur[0] = 0

    i_lo = lax.while_loop(lambda i: starts_ref[0, 0, i + 1] <= slot0, lambda i: i + 1, cur[0])
    i_hi = lax.while_loop(lambda i: jnp.logical_and(i + 1 < nt, starts_ref[0, 0, i + 1] < slot0 + ts),
                          lambda i: i + 1, i_lo)
    cur[0] = i_lo
    tiles_per_chunk = GATHER_CHUNK // LANES_V7X
    c_lo = i_lo // tiles_per_chunk
    c_hi = i_hi // tiles_per_chunk

    def chunk_copy(c, b):
        return pltpu.make_async_copy(h_hbm.at[pl.ds(c * GATHER_CHUNK, GATHER_CHUNK), :], hbuf.at[b], sem.at[b])

    chunk_copy(c_lo, 0).start()
    xe_ref[...] = jnp.zeros(xe_ref.shape, BF16)
    gsub = min(LANES_V7X, ts)
    sub_iota = lax.broadcasted_iota(I32, (gsub, LANES_V7X), 0)

    def body(c, carry):
        b = (c - c_lo) % 2
        chunk_copy(c, b).wait()

        @pl.when(c < c_hi)
        def _():
            chunk_copy(c + 1, 1 - b).start()

        s_lo = starts_ref[0, 0, c * tiles_per_chunk]
        s_hi = starts_ref[0, 0, (c + 1) * tiles_per_chunk]
        ps = [pos_ref[0, pl.ds(c * tiles_per_chunk + j, 1), :] for j in range(tiles_per_chunk)]
        for si in range(ts // gsub):
            lo = slot0 + si * gsub

            @pl.when(jnp.logical_and(s_lo < lo + gsub, s_hi > lo))
            def _(si=si, lo=lo):
                ids = lo + sub_iota
                onehot = jnp.concatenate([jnp.where(p == ids, 1.0, 0.0) for p in ps], axis=1).astype(BF16)
                rows = slice(si * gsub, (si + 1) * gsub)
                xe_ref[rows, :] += jnp.dot(onehot, hbuf[b], preferred_element_type=F32).astype(BF16)
        return carry

    lax.fori_loop(c_lo, c_hi + 1, body, 0)
    x = xe_ref[...]
    g = jnp.dot(x, wg_ref[0], preferred_element_type=F32)
    u = jnp.dot(x, wu_ref[0], preferred_element_type=F32)
    a = ((g * _sigmoid(g)) * u).astype(BF16)
    o_ref[...] = jnp.dot(a, wd_ref[0], preferred_element_type=F32).astype(BF16)


def _ffn(h2, pos_e, starts_e, wg, wu, wd, cap, ts):
    n_tok, D = h2.shape
    E = N_EXPERTS
    kt = cap // ts
    nt = pos_e.shape[1]
    F = D_EXPERT
    return pl.pallas_call(
        functools.partial(_ffn_body, ts, kt, nt),
        out_shape=jax.ShapeDtypeStruct((E * cap, D), BF16),
        grid=(E, kt),
        in_specs=[
            pl.BlockSpec((1, 1, nt + 1), lambda e, k: (e, 0, 0), memory_space=pltpu.SMEM),
            pl.BlockSpec((1, nt, LANES_V7X), lambda e, k: (e, 0, 0)),
            pl.BlockSpec(memory_space=pl.ANY),
            pl.BlockSpec((1, D, F), lambda e, k: (e, 0, 0)),
            pl.BlockSpec((1, D, F), lambda e, k: (e, 0, 0)),
            pl.BlockSpec((1, F, D), lambda e, k: (e, 0, 0)),
        ],
        out_specs=pl.BlockSpec((ts, D), lambda e, k: (e * kt + k, 0)),
        scratch_shapes=[pltpu.VMEM((ts, D), BF16), pltpu.VMEM((2, GATHER_CHUNK, D), BF16),
                        pltpu.SemaphoreType.DMA((2,)), pltpu.SMEM((1,), I32)],
        compiler_params=_cparams(2), name="ffn",
    )(starts_e, pos_e, h2, wg, wu, wd)


SLOT_CHUNK = LANES_V7X


def _combine_body(cap, st_ref, en_ref, x_ref, mod_ref, aff_ref, pos_ref, ye_hbm, o_ref, ybuf, sem, c_exp, c_base):
    E = N_EXPERTS
    tc = x_ref.shape[1]
    n_rows = E * cap
    sub = tc // LANES_V7X

    def chunk_copy(row0, idx):
        return pltpu.make_async_copy(ye_hbm.at[pl.ds(row0, SLOT_CHUNK), :], ybuf.at[idx], sem.at[0])

    def add_chunk(e, row0, base, idx):
        c_exp[idx] = e
        c_base[idx] = base
        chunk_copy(row0, idx).start()
        return idx + 1

    def issue_expert(e, idx):
        st, en = st_ref[0, 0, e], en_ref[0, 0, e]
        q0 = st // SLOT_CHUNK
        q1 = jnp.where(en > st, (en + SLOT_CHUNK - 1) // SLOT_CHUNK, q0)

        def one(q, idx):
            row0 = jnp.minimum(e * cap + q * SLOT_CHUNK, n_rows - SLOT_CHUNK)
            return add_chunk(e, row0, row0 - e * cap, idx)
        return lax.fori_loop(q0, q1, one, idx)

    total = lax.fori_loop(0, E, issue_expert, jnp.int32(0))
    total = lax.cond(total % 2 == 1, lambda t: add_chunk(0, 0, NO_SLOT // 2, t), lambda t: t, total)

    def wait_one(j, c):
        chunk_copy(0, j).wait()
        return c
    lax.fori_loop(0, total, wait_one, 0)

    slot_iota = lax.broadcasted_iota(I32, (SLOT_CHUNK, LANES_V7X), 0)

    def weighted_onehot_t(idx):
        e, base = c_exp[idx], c_base[idx]
        parts = []
        for j in range(sub):
            p = pos_ref[j, pl.ds(e, 1), :]
            w = aff_ref[j, pl.ds(e, 1), :]
            parts.append(jnp.where(p == base + slot_iota, w, 0.0))
        return jnp.concatenate(parts, axis=1)

    def pair(pi, acc):
        i0 = 2 * pi
        wt = jnp.concatenate([weighted_onehot_t(i0), weighted_onehot_t(i0 + 1)], axis=0).T
        hi = wt.astype(BF16)
        lo = (wt - hi.astype(F32)).astype(BF16)
        y = jnp.concatenate([ybuf[i0], ybuf[i0 + 1]], axis=0)
        return acc + jnp.dot(hi, y, preferred_element_type=F32) + jnp.dot(lo, y, preferred_element_type=F32)

    acc = lax.fori_loop(0, total // 2, pair, jnp.zeros((tc, D_MODEL), F32))
    m = mod_ref[0]
    o_ref[0] = x_ref[0] + m[:, 5 * D_MODEL:6 * D_MODEL] * acc


def _combine(x1, mod, aff3, ye, pos3, st_t, en_t, cap, tc):
    B, n, D = x1.shape
    nt = n // tc
    sub = tc // LANES_V7X
    mod_b = mod.shape[0]
    E = N_EXPERTS
    max_chunks = E * (tc // SLOT_CHUNK + 1) + 2
    return pl.pallas_call(
        functools.partial(_combine_body, cap),
        out_shape=jax.ShapeDtypeStruct((B, n, D), F32),
        grid=(B, nt),
        in_specs=[
            pl.BlockSpec((1, 1, E), lambda b, i: (b * nt + i, 0, 0), memory_space=pltpu.SMEM),
            pl.BlockSpec((1, 1, E), lambda b, i: (b * nt + i, 0, 0), memory_space=pltpu.SMEM),
            pl.BlockSpec((1, tc, D), lambda b, i: (b, i, 0)),
            pl.BlockSpec((1, 1, 6 * D), (lambda b, i: (b, 0, 0)) if mod_b > 1 else (lambda b, i: (0, 0, 0))),
            pl.BlockSpec((sub, E, LANES_V7X), lambda b, i: (b * nt + i, 0, 0)),
            pl.BlockSpec((sub, E, LANES_V7X), lambda b, i: (b * nt + i, 0, 0)),
            pl.BlockSpec(memory_space=pl.ANY),
        ],
        out_specs=pl.BlockSpec((1, tc, D), lambda b, i: (b, i, 0)),
        scratch_shapes=[pltpu.VMEM((max_chunks, SLOT_CHUNK, D), BF16), pltpu.SemaphoreType.DMA((1,)),
                        pltpu.SMEM((max_chunks,), I32), pltpu.SMEM((max_chunks,), I32)],
        compiler_params=_cparams(2), name="combine",
    )(st_t, en_t, x1, mod, aff3, pos3, ye)


def _expert_choice_ffn(x1, h2, aff3, mod, P, l):
    B, n, D = x1.shape
    E = N_EXPERTS
    n_tok = B * n
    cap = EC_FACTOR * n_tok // E
    nt = n_tok // LANES_V7X
    aff3 = aff3.reshape(nt, E, LANES_V7X)
    pos3, starts3 = _route(aff3, cap)
    starts = starts3[:, :, 0]
    pos_e = jnp.transpose(pos3, (1, 0, 2))
    starts_e = jnp.concatenate([starts.T, jnp.full((E, 1), cap, I32)], axis=1).reshape(E, 1, nt + 1)
    ts = min(512, cap)
    ye = _ffn(h2.reshape(n_tok, D), pos_e, starts_e, P["wgate"][l], P["wup"][l], P["wdown"][l], cap, ts)
    tc = min(512, n)
    sub = tc // LANES_V7X
    st_t = starts[::sub]
    en_t = jnp.concatenate([starts[sub::sub], jnp.full((1, E), cap, I32)], axis=0)
    ntc = n_tok // tc
    return _combine(x1, mod, aff3, ye, pos3, st_t.reshape(ntc, 1, E), en_t.reshape(ntc, 1, E), cap, tc)


def _layer(x, mod, P, l, ctx, rope_tabs):
    B, n, _ = x.shape
    latent = ctx is not None
    t_tile = 256
    outs = _proj(x, mod, P, l, rope_tabs if latent else None, not latent, t_tile)
    qa, ka, va, qb, kb, vb, qc, kc, vc, qd, kd, vd = outs[:12]
    if latent:
        c_ak, c_av, c_bk, c_bv, c_kc, c_vc, c_dk, c_dv = ctx
        tq, tk = 512, 512
    else:
        c_ak = c_av = c_bk = c_bv = c_kc = c_vc = c_dk = c_dv = None
        tq, tk = n, n
    lam_init = _lambda_init(l)
    oa = _attention(qa, ka, va, c_ak, c_av, scale=SCALE_A, kmap=lambda g: g, vmap=lambda g: g, tq=tq, tk=tk,
                    diff=lam_init, lam=P["alam"][l], subln=P["asubln_col"][l], name="attn_a")
    ob = _attention(qb, kb, vb, c_bk, c_bv, scale=SCALE_B, kmap=lambda g: g // 2, vmap=lambda g: g // 2,
                    tq=tq, tk=tk, name="attn_b")
    oc = _attention(qc, kc, vc, c_kc, c_vc, scale=SCALE_C, kmap=lambda g: g, vmap=lambda g: g, tq=tq, tk=tk,
                    name="attn_c")
    od = _attention(qd, kd, vd, c_dk, c_dv, scale=SCALE_D, kmap=lambda g: g // 2, vmap=lambda g: g // 2,
                    tq=tq, tk=tk, window=WINDOW if latent else None, sink=P["dsink"][l], transposed=False,
                    name="attn_d")
    x1, h2, aff3 = _merge(x, mod, P, l, (oa, ob, oc, od), t_tile)
    x2 = _expert_choice_ffn(x1, h2, aff3, mod, P, l)
    return x2, outs[12:]


def _rope_tables(n, dim):
    rows = n // GRID_W
    row = jnp.repeat(jnp.arange(rows, dtype=F32), GRID_W)
    col = jnp.tile(jnp.arange(GRID_W, dtype=F32), rows)
    nf = dim // 4
    inv = ROPE_THETA ** (-jnp.arange(nf, dtype=F32) / nf)
    ar = row[:, None] * inv
    ac = col[:, None] * inv
    ang = jnp.concatenate([ar, ar, ac, ac], axis=-1)
    cos, sin = jnp.cos(ang), jnp.sin(ang)
    first = (jnp.arange(dim) % (2 * nf)) < nf
    sin_a = jnp.where(first[None, :], -sin, 0.0)
    sin_b = jnp.where(first[None, :], 0.0, sin)
    reps = LANES_V7X // dim
    return tuple(jnp.tile(t, (1, reps)) for t in (cos, sin_a, sin_b))


def _prepare_params(ada_w, ada_b, norm1_g, norm2_g, w_in, a_q_norm, a_k_norm, a_lambda, a_subln, b_q_norm,
                    b_k_norm, c_qa_norm, c_kva_norm, c_wq_up, c_wkv_up, c_q_norm, c_k_norm, c_qr_norm,
                    c_kr_norm, d_q_norm, d_k_norm, d_sink, w_branch, w_out, w_router, w_gate, w_up, w_down):
    L = w_in.shape[0]

    def row(a):
        return a[:, None, :].astype(F32)

    def tiled(a, reps):
        return jnp.tile(a, (1, reps))[:, None, :].astype(F32)

    w1 = jnp.concatenate([w_in[:, :, :_CKR_END_ORIG],
                          jnp.zeros((L, D_MODEL, _C_CKR[1] - _C_CKR[0] - ROPE_C), w_in.dtype),
                          w_in[:, :, _CKR_END_ORIG:_W_ORIG_NOGATE]], axis=-1).astype(BF16)
    wq = c_wq_up.reshape(L, Q_LORA, H_C, NOPE_C + ROPE_C)
    wqup = jnp.concatenate([wq[..., :NOPE_C].reshape(L, Q_LORA, H_C * NOPE_C),
                            wq[..., NOPE_C:].reshape(L, Q_LORA, H_C * ROPE_C)], axis=-1).astype(BF16)
    wkv = c_wkv_up.reshape(L, KV_LORA, H_C, NOPE_C + V_C)
    wkvup = jnp.concatenate([wkv[..., :NOPE_C].reshape(L, KV_LORA, H_C * NOPE_C),
                             wkv[..., NOPE_C:].reshape(L, KV_LORA, H_C * V_C)], axis=-1).astype(BF16)
    ckrn = jnp.concatenate([c_kr_norm, jnp.zeros((L, LANES_V7X - ROPE_C), c_kr_norm.dtype)], axis=-1)
    return dict(
        g1=row(norm1_g), g2=row(norm2_g), w1=w1,
        aqn=tiled(a_q_norm, 2 * H_A), akn=tiled(a_k_norm, 2 * H_A),
        bqn=tiled(b_q_norm, H_B), bkn=tiled(b_k_norm, KV_B),
        cqan=row(c_qa_norm), ckvan=row(c_kva_norm), wqup=wqup, wkvup=wkvup,
        cqn=tiled(c_q_norm, H_C), ckn=tiled(c_k_norm, H_C), cqrn=tiled(c_qr_norm, H_C), ckrn=row(ckrn),
        dqn=tiled(d_q_norm, H_D), dkn=tiled(d_k_norm, KV_D),
        alam=a_lambda.astype(F32), asubln_col=a_subln[:, :, None].astype(F32), dsink=d_sink.astype(F32),
        wgates=w_in[:, :, _W_ORIG_NOGATE:].astype(BF16), wbranch=w_branch.astype(BF16), wout=w_out.astype(BF16),
        wrouter_t=jnp.transpose(w_router, (0, 2, 1)).astype(BF16),
        wgate=w_gate.astype(BF16), wup=w_up.astype(BF16), wdown=w_down.astype(BF16),
    )


def _head_major(c):
    return jnp.transpose(c, (0, 2, 1, 3)).astype(BF16)


def _head_major_t(c):
    return jnp.transpose(c, (0, 2, 3, 1)).astype(BF16)


def kernel(x_prompt, x_sample, c, cache_a_k, cache_a_v, cache_b_k, cache_b_v, cache_c_kv, cache_c_kr, cache_d_k, cache_d_v, c_ctx, ada_w, ada_b, norm1_g, norm2_g, w_in, a_q_norm, a_k_norm, a_lambda, a_subln, b_q_norm, b_k_norm, c_qa_norm, c_kva_norm, c_wq_up, c_wkv_up, c_q_norm, c_k_norm, c_qr_norm, c_kr_norm, d_q_norm, d_k_norm, d_sink, w_branch, w_out, w_router, w_gate, w_up, w_down):
    L = w_in.shape[0]
    P = _prepare_params(ada_w, ada_b, norm1_g, norm2_g, w_in, a_q_norm, a_k_norm, a_lambda, a_subln, b_q_norm,
                        b_k_norm, c_qa_norm, c_kva_norm, c_wq_up, c_wkv_up, c_q_norm, c_k_norm, c_qr_norm,
                        c_kr_norm, d_q_norm, d_k_norm, d_sink, w_branch, w_out, w_router, w_gate, w_up, w_down)
    dec_b = c.shape[0]
    rows = 8 * ((1 + dec_b + 7) // 8)
    cond = jnp.concatenate([c_ctx[None, :], c, jnp.zeros((rows - 1 - dec_b, D_MODEL), F32)], axis=0)
    mod = _modulation(cond, ada_w, ada_b)
    mod_ctx = mod[:, 0:1, None, :]
    mod_lat = mod[:, 1:1 + dec_b, None, :]

    x = x_prompt
    caches = []
    for l in range(L):
        x, cache = _layer(x, mod_ctx[l], P, l, None, None)
        caches.append(cache)
    y_prompt = x
    B, n = x_prompt.shape[:2]
    new = [jnp.stack([lc[j] for lc in caches], axis=1) for j in range(8)]
    new_a_k = new[0].reshape(B, L, n, 2 * H_A, HD_A)
    new_a_v = new[1].reshape(B, L, n, H_A, 2 * HD_A)
    new_b_k = new[2].reshape(B, L, n, KV_B, HD_B)
    new_b_v = new[3].reshape(B, L, n, KV_B, HD_B)
    new_c_kv, new_c_kr = new[4], new[5]
    new_d_k = new[6].reshape(B, L, n, KV_D, HD_D)
    new_d_v = new[7].reshape(B, L, n, KV_D, HD_D)

    n_lat = x_sample.shape[1]
    rope_tabs = _rope_tables(n_lat, HD_A) + _rope_tables(n_lat, ROPE_C)
    x = x_sample
    for l in range(L):
        kc_ctx, vc_ctx = _mla_cache(cache_c_kv[:, l], cache_c_kr[:, l], P["wkvup"][l], P["ckn"][l])
        ctx = (_head_major(cache_a_k[:, l]), _head_major_t(cache_a_v[:, l]),
               _head_major(cache_b_k[:, l]), _head_major_t(cache_b_v[:, l]),
               kc_ctx, vc_ctx,
               _head_major(cache_d_k[:, l]), _head_major(cache_d_v[:, l]))
        x, _ = _layer(x, mod_lat[l], P, l, ctx, rope_tabs)
    y_sample = x
    return (y_prompt, y_sample, new_a_k, new_a_v, new_b_k, new_b_v, new_c_kv, new_c_kr, new_d_k, new_d_v)
```

```python
import functools
import math

import numpy as np
import jax
import jax.numpy as jnp
from jax import lax
from jax.experimental import pallas as pl
from jax.experimental.pallas import tpu as pltpu

F32 = jnp.float32
BF16 = jnp.bfloat16
I32 = jnp.int32

D_MODEL = 1024
GRID_W = 64
ROPE_THETA = 10000.0
EPS = 1e-6
NEG_BIG = -1e30
LOG2E = 1.4426950408889634

H_A, HD_A = 4, 64
H_B, KV_B, HD_B = 8, 2, 64
H_C, Q_LORA, KV_LORA, NOPE_C, ROPE_C, V_C = 8, 384, 256, 64, 32, 64
H_D, KV_D, HD_D = 8, 2, 64
WINDOW = 128
N_BRANCH, BRANCH_W = 4, 512
N_EXPERTS, D_EXPERT, EC_FACTOR = 16, 1408, 2

SCALE_A = HD_A ** -0.5
SCALE_B = HD_B ** -0.5
SCALE_C = (NOPE_C + ROPE_C) ** -0.5
SCALE_D = HD_D ** -0.5

LANES_V7X = 128
VMEM_LIMIT_BYTES_V7X = 56 * 1024 * 1024

_W_ORIG_NOGATE = 3744
_C_AQ, _C_AK, _C_AV = (0, 512), (512, 1024), (1024, 1536)
_C_BQ, _C_BK, _C_BV = (1536, 2048), (2048, 2176), (2176, 2304)
_C_CQ, _C_CKV, _C_CKR = (2304, 2688), (2688, 2944), (2944, 3072)
_C_DQ, _C_DK, _C_DV = (3072, 3584), (3584, 3712), (3712, 3840)
_W1_COLS = 3840
_CKR_END_ORIG = 2976

NO_SLOT = -(2 ** 30)


def _lambda_init(l):
    return 0.8 - 0.6 * math.exp(-0.3 * l)


def _cparams(n_axes):
    return pltpu.CompilerParams(
        dimension_semantics=("arbitrary",) * n_axes,
        vmem_limit_bytes=VMEM_LIMIT_BYTES_V7X,
    )


def _const_spec(shape):
    nd = len(shape)
    return pl.BlockSpec(shape, lambda *_: (0,) * nd)


def _inv_rms(z):
    return lax.rsqrt(jnp.mean(z * z, axis=-1, keepdims=True) + EPS)


def _sigmoid(x):
    return 1.0 / (1.0 + jnp.exp(-x))


def _mod_body(c_ref, w_ref, b_ref, o_ref):
    c = c_ref[...]
    s = (c * _sigmoid(c)).astype(BF16)
    o_ref[0] = jnp.dot(s, w_ref[0].astype(BF16), preferred_element_type=F32) + b_ref[0]


def _modulation(cond, ada_w, ada_b):
    L = ada_w.shape[0]
    R = cond.shape[0]
    nblk = 6
    return pl.pallas_call(
        _mod_body,
        out_shape=jax.ShapeDtypeStruct((L, R, 6 * D_MODEL), F32),
        grid=(L, nblk),
        in_specs=[
            pl.BlockSpec((R, D_MODEL), lambda l, j: (0, 0)),
            pl.BlockSpec((1, D_MODEL, D_MODEL), lambda l, j: (l, 0, j)),
            pl.BlockSpec((1, 1, D_MODEL), lambda l, j: (l, 0, j)),
        ],
        out_specs=pl.BlockSpec((1, R, D_MODEL), lambda l, j: (l, 0, j)),
        compiler_params=_cparams(2),
        name="modulation",
    )(cond, ada_w, ada_b.reshape(L, 1, 6 * D_MODEL))


def _ada_norm(x, g, shift, scale):
    return (x * _inv_rms(x) * g) * (1.0 + scale) + shift


def _rotary(zg, cos, sin_a, sin_b, nf):
    w = zg.shape[-1]
    return zg * cos + pltpu.roll(zg, w - nf, 1) * sin_a + pltpu.roll(zg, nf, 1) * sin_b


def _tile_lanes(t, width):
    reps = width // t.shape[-1]
    return t if reps == 1 else jnp.concatenate([t] * reps, axis=1)


def _head_norm(z, g_tiled, n_heads, hd, rope):
    zg = z * g_tiled
    if rope is not None:
        cos, sin_a, sin_b, nf = rope
        w = z.shape[-1]
        zg = _rotary(zg, _tile_lanes(cos, w), _tile_lanes(sin_a, w), _tile_lanes(sin_b, w), nf)
    outs = []
    for h in range(n_heads):
        zh = z[:, h * hd:(h + 1) * hd]
        outs.append(zg[:, h * hd:(h + 1) * hd] * _inv_rms(zh))
    return outs


def _mla_expand(ckv_n, kr, wkv_ref, ckn_tiled, kc_ref, vc_ref):
    kv = jnp.dot(ckv_n.astype(BF16), wkv_ref[...], preferred_element_type=F32)
    kn = _head_norm(kv[:, :H_C * NOPE_C], ckn_tiled, H_C, NOPE_C, None)
    krb = kr.astype(BF16)
    t = kv.shape[0]
    pad = jnp.zeros((t, LANES_V7X - NOPE_C - ROPE_C), BF16)
    for h in range(H_C):
        kc_ref[0, h, :, 0:NOPE_C] = kn[h].astype(BF16)
        kc_ref[0, h, :, NOPE_C:NOPE_C + ROPE_C] = krb
        kc_ref[0, h, :, NOPE_C + ROPE_C:LANES_V7X] = pad
    vt = kv[:, H_C * NOPE_C:].astype(BF16).T
    for h in range(H_C):
        vc_ref[0, h] = vt[h * V_C:(h + 1) * V_C, :]


def _proj_body(rope, emit_cache, *refs):
    it = iter(refs)
    x_ref, mod_ref, g1_ref, w1_ref = next(it), next(it), next(it), next(it)
    aqn, akn, bqn, bkn = next(it), next(it), next(it), next(it)
    cqan, ckvan, wqup_ref, wkvup_ref = next(it), next(it), next(it), next(it)
    cqn, ckn, cqrn, ckrn, dqn, dkn = next(it), next(it), next(it), next(it), next(it), next(it)
    if rope:
        cos64, sa64, sb64, cos32, sa32, sb32 = (next(it)[...] for _ in range(6))
        rope64 = (cos64, sa64, sb64, HD_A // 4)
        rope32 = (cos32, sa32, sb32, ROPE_C // 4)
    else:
        rope64 = rope32 = None
    qa_ref, ka_ref, va_ref = next(it), next(it), next(it)
    qb_ref, kb_ref, vb_ref = next(it), next(it), next(it)
    qc_ref, kc_ref, vc_ref = next(it), next(it), next(it)
    qd_ref, kd_ref, vd_ref = next(it), next(it), next(it)
    if emit_cache:
        c_ak, c_av, c_bk, c_bv, c_ckv, c_ckr, c_dk, c_dv = (next(it) for _ in range(8))

    x = x_ref[0]
    m = mod_ref[0]
    hb = _ada_norm(x, g1_ref[...], m[:, 0:D_MODEL], m[:, D_MODEL:2 * D_MODEL]).astype(BF16)
    t = x.shape[0]

    def seg(c):
        return jnp.dot(hb, w1_ref[:, c[0]:c[1]], preferred_element_type=F32)

    def store_heads(ys, ref):
        for h, y in enumerate(ys):
            ref[0, h] = y.astype(BF16)

    def store_cache(ys, ref, hd):
        for h, y in enumerate(ys):
            ref[0, :, h * hd:(h + 1) * hd] = y

    store_heads(_head_norm(seg(_C_AQ), aqn[...], 2 * H_A, HD_A, rope64), qa_ref)
    ys = _head_norm(seg(_C_AK), akn[...], 2 * H_A, HD_A, rope64)
    store_heads(ys, ka_ref)
    if emit_cache:
        store_cache(ys, c_ak, HD_A)
    av = seg(_C_AV)
    avt = av.astype(BF16).T
    for h in range(H_A):
        va_ref[0, h] = avt[h * 2 * HD_A:(h + 1) * 2 * HD_A, :]
    if emit_cache:
        c_av[0] = av

    for (cq, ck, cv, qn, kn_, q_ref, k_ref, v_ref, ck_ref, cv_ref) in (
        (_C_BQ, _C_BK, _C_BV, bqn, bkn, qb_ref, kb_ref, vb_ref,
         c_bk if emit_cache else None, c_bv if emit_cache else None),
        (_C_DQ, _C_DK, _C_DV, dqn, dkn, qd_ref, kd_ref, vd_ref,
         c_dk if emit_cache else None, c_dv if emit_cache else None),
    ):
        store_heads(_head_norm(seg(cq), qn[...], H_B, HD_B, rope64), q_ref)
        ys = _head_norm(seg(ck), kn_[...], KV_B, HD_B, rope64)
        store_heads(ys, k_ref)
        v = seg(cv)
        if v_ref is vb_ref:
            vt = v.astype(BF16).T
            for h in range(KV_B):
                v_ref[0, h] = vt[h * HD_B:(h + 1) * HD_B, :]
        else:
            for h in range(KV_B):
                v_ref[0, h] = v[:, h * HD_B:(h + 1) * HD_B].astype(BF16)
        if emit_cache:
            store_cache(ys, ck_ref, HD_B)
            cv_ref[0] = v

    cq = seg(_C_CQ)
    cq_n = (cq * _inv_rms(cq) * cqan[...]).astype(BF16)
    zq = jnp.dot(cq_n, wqup_ref[...], preferred_element_type=F32)
    q_nope = _head_norm(zq[:, :H_C * NOPE_C], cqn[...], H_C, NOPE_C, None)
    q_rope = _head_norm(zq[:, H_C * NOPE_C:], cqrn[...], H_C, ROPE_C, rope32)
    pad = jnp.zeros((t, LANES_V7X - NOPE_C - ROPE_C), BF16)
    for h in range(H_C):
        qc_ref[0, h, :, 0:NOPE_C] = q_nope[h].astype(BF16)
        qc_ref[0, h, :, NOPE_C:NOPE_C + ROPE_C] = q_rope[h].astype(BF16)
        qc_ref[0, h, :, NOPE_C + ROPE_C:LANES_V7X] = pad

    ckv = seg(_C_CKV)
    ckv_n = ckv * _inv_rms(ckv) * ckvan[...]
    ckr = seg(_C_CKR)
    kr_plain = ckr * ckrn[...] * _inv_rms(ckr[:, :ROPE_C])
    if emit_cache:
        c_ckv[0] = ckv_n
        c_ckr[0] = kr_plain[:, :ROPE_C]
    if rope:
        zg = ckr * ckrn[...]
        kr = _rotary(zg, cos32, sa32, sb32, ROPE_C // 4) * _inv_rms(ckr[:, :ROPE_C])
    else:
        kr = kr_plain
    _mla_expand(ckv_n, kr[:, :ROPE_C], wkvup_ref, ckn[...], kc_ref, vc_ref)


def _proj(x, mod, P, l, rope_tabs, emit_cache, t_tile):
    B, n, _ = x.shape
    nt = n // t_tile
    mod_b = mod.shape[0]
    grid = (B, nt)

    def tok_spec(w):
        return pl.BlockSpec((1, t_tile, w), lambda b, i: (b, i, 0))

    def head_spec(h, d):
        return pl.BlockSpec((1, h, t_tile, d), lambda b, i: (b, 0, i, 0))

    small = [P["aqn"][l], P["akn"][l], P["bqn"][l], P["bkn"][l], P["cqan"][l], P["ckvan"][l],
             P["wqup"][l], P["wkvup"][l], P["cqn"][l], P["ckn"][l], P["cqrn"][l], P["ckrn"][l],
             P["dqn"][l], P["dkn"][l]]
    ins = [x, mod, P["g1"][l], P["w1"][l]] + small
    in_specs = [
        tok_spec(D_MODEL),
        pl.BlockSpec((1, 1, 6 * D_MODEL), (lambda b, i: (b, 0, 0)) if mod_b > 1 else (lambda b, i: (0, 0, 0))),
        _const_spec((1, D_MODEL)),
        pl.BlockSpec((D_MODEL, _W1_COLS), lambda b, i: (0, 0), pipeline_mode=pl.Buffered(1)),
    ] + [_const_spec(a.shape) for a in small]
    rope = rope_tabs is not None
    if rope:
        ins += list(rope_tabs)
        in_specs += [pl.BlockSpec((t_tile, LANES_V7X), lambda b, i: (i, 0)) for _ in rope_tabs]

    def hs(h, d):
        return jax.ShapeDtypeStruct((B, h, n, d), BF16)

    def hst(h, d):
        return jax.ShapeDtypeStruct((B, h, d, n), BF16)

    def head_spec_t(h, d):
        return pl.BlockSpec((1, h, d, t_tile), lambda b, i: (b, 0, 0, i))

    out_shape = [hs(8, 64), hs(8, 64), hst(4, 128), hs(8, 64), hs(2, 64), hst(2, 64),
                 hs(8, 128), hs(8, 128), hst(8, 64), hs(8, 64), hs(2, 64), hs(2, 64)]
    out_specs = [head_spec(8, 64), head_spec(8, 64), head_spec_t(4, 128), head_spec(8, 64), head_spec(2, 64),
                 head_spec_t(2, 64), head_spec(8, 128), head_spec(8, 128), head_spec_t(8, 64), head_spec(8, 64),
                 head_spec(2, 64), head_spec(2, 64)]
    if emit_cache:
        for w in (512, 512, 128, 128, KV_LORA, ROPE_C, 128, 128):
            out_shape.append(jax.ShapeDtypeStruct((B, n, w), F32))
            out_specs.append(tok_spec(w))
    return pl.pallas_call(
        functools.partial(_proj_body, rope, emit_cache),
        out_shape=out_shape, grid=grid, in_specs=in_specs, out_specs=out_specs,
        compiler_params=_cparams(2), name="proj",
    )(*ins)


def _mla_cache_body(ckv_ref, ckr_ref, wkvup_ref, ckn_ref, kc_ref, vc_ref):
    _mla_expand(ckv_ref[0], ckr_ref[0], wkvup_ref, ckn_ref[...], kc_ref, vc_ref)


def _mla_cache(ckv, ckr, wkvup, ckn_tiled):
    B, K, _ = ckv.shape
    return pl.pallas_call(
        _mla_cache_body,
        out_shape=[jax.ShapeDtypeStruct((B, H_C, K, LANES_V7X), BF16),
                   jax.ShapeDtypeStruct((B, H_C, V_C, K), BF16)],
        grid=(B,),
        in_specs=[pl.BlockSpec((1, K, KV_LORA), lambda b: (b, 0, 0)),
                  pl.BlockSpec((1, K, ROPE_C), lambda b: (b, 0, 0)),
                  _const_spec(wkvup.shape), _const_spec(ckn_tiled.shape)],
        out_specs=[pl.BlockSpec((1, H_C, K, LANES_V7X), lambda b: (b, 0, 0, 0)),
                   pl.BlockSpec((1, H_C, V_C, K), lambda b: (b, 0, 0, 0))],
        compiler_params=_cparams(1), name="mla_cache",
    )(ckv, ckr, wkvup, ckn_tiled)


def _attn_body(cfg, *refs):
    it = iter(refs)
    q_ref = next(it)
    kc_ref = vc_ref = None
    if cfg["has_ctx"]:
        kc_ref, vc_ref = next(it), next(it)
    ko_ref, vo_ref = next(it), next(it)
    sink_ref = next(it) if cfg["sink"] else None
    if cfg["diff"] is not None:
        lam_ref, subln_ref = next(it), next(it)
    o_ref = next(it)

    R, RK, RV = cfg["R"], cfg["RK"], cfg["RV"]
    tq, tk, ko = cfg["tq"], cfg["tk"], cfg["ko"]
    c = cfg["scale"] * LOG2E
    dv = vo_ref.shape[-2] if cfg["transposed"] else vo_ref.shape[-1]
    g = pl.program_id(1)
    i = pl.program_id(2)
    nt_dims = (((1,), (1,)), ((), ()))
    rks = [r if RK > 1 else 0 for r in range(R)]
    rvs = [r if RV > 1 else 0 for r in range(R)]
    qs = [(q_ref[0, r].astype(F32) * c).astype(BF16) for r in range(R)]

    tr = cfg["transposed"]
    red = 0 if tr else -1

    def scores(q, k):
        if tr:
            return lax.dot_general(k, q, nt_dims, preferred_element_type=F32)
        return lax.dot_general(q, k, nt_dims, preferred_element_type=F32)

    def update(s, v, carry, mask):
        m, l, acc = carry
        if mask is not None:
            s = jnp.where(mask, s, NEG_BIG)
        m_new = jnp.maximum(m, jnp.max(s, axis=red, keepdims=True))
        alpha = jnp.exp2(m - m_new)
        p = jnp.exp2(s - m_new)
        l = alpha * l + jnp.sum(p, axis=red, keepdims=True)
        if tr:
            acc = alpha * acc + jnp.dot(v, p.astype(BF16), preferred_element_type=F32)
        else:
            acc = alpha * acc + jnp.dot(p.astype(BF16), v, preferred_element_type=F32)
        return m_new, l, acc

    def kv(k_ref, v_ref, r, st, w):
        if st is None:
            return k_ref[0, rks[r]], v_ref[0, rvs[r]]
        if tr:
            return k_ref[0, rks[r], pl.ds(st, w), :], v_ref[0, rvs[r], :, pl.ds(st, w)]
        return k_ref[0, rks[r], pl.ds(st, w), :], v_ref[0, rvs[r], pl.ds(st, w), :]

    def run_chunks(chunks, carries):
        def issue(ch):
            k_ref, v_ref, st, w, mask = ch
            kvs = [kv(k_ref, v_ref, r, st, w) for r in range(R)]
            return [scores(qs[r], kvs[r][0]) for r in range(R)], [x[1] for x in kvs], mask
        nxt = issue(chunks[0])
        for ci in range(len(chunks)):
            ss, vs, mask = nxt
            if ci + 1 < len(chunks):
                nxt = issue(chunks[ci + 1])
            carries = tuple(update(ss[r], vs[r], carries[r], mask) for r in range(R))
        return carries

    stat = (1, tq) if tr else (tq, 1)
    carries = []
    for r in range(R):
        if cfg["sink"]:
            m0 = jnp.full(stat, sink_ref[g * R + r] * LOG2E, F32)
            l0 = jnp.ones(stat, F32)
        else:
            m0 = jnp.full(stat, NEG_BIG, F32)
            l0 = jnp.zeros(stat, F32)
        carries.append((m0, l0, jnp.zeros((dv, tq) if tr else (tq, dv), F32)))
    chunks = []
    if cfg["has_ctx"]:
        chunks.append((kc_ref, vc_ref, None, None, None))
    if cfg["window"] is not None:
        win = cfg["window"]
        wk = min(tq + 2 * win, ko)
        q0 = i * tq
        k0 = pl.multiple_of(jnp.clip(q0 - win, 0, ko - wk), win)
        qpos = q0 + lax.broadcasted_iota(I32, (tq, wk), 0)
        kpos = k0 + lax.broadcasted_iota(I32, (tq, wk), 1)
        chunks.append((ko_ref, vo_ref, k0, wk, jnp.abs(qpos - kpos) <= win))
    elif ko == tk:
        chunks.append((ko_ref, vo_ref, None, None, None))
    else:
        for j in range(ko // tk):
            chunks.append((ko_ref, vo_ref, j * tk, tk, None))
    carries = run_chunks(chunks, tuple(carries))
    outs = [acc / l for (_, l, acc) in carries]

    if cfg["diff"] is not None:
        lp = lam_ref[...]
        lam = (jnp.exp(jnp.sum(lp[0:1] * lp[1:2], axis=-1, keepdims=True))
               - jnp.exp(jnp.sum(lp[2:3] * lp[3:4], axis=-1, keepdims=True)) + cfg["diff"])
        d = outs[0] - lam * outs[1]
        if tr:
            inv = lax.rsqrt(jnp.mean(d * d, axis=0, keepdims=True) + EPS)
            o_ref[0] = (d * inv * subln_ref[...] * (1.0 - cfg["diff"])).T.astype(BF16)
        else:
            o_ref[0] = (d * _inv_rms(d) * subln_ref[...] * (1.0 - cfg["diff"])).astype(BF16)
    elif tr:
        o_ref[0] = jnp.concatenate(outs, axis=0).T.astype(BF16)
    else:
        for r in range(R):
            o_ref[0, :, r * dv:(r + 1) * dv] = outs[r].astype(BF16)


def _attention(q, ko, vo, kc, vc, *, scale, kmap, vmap, tq, tk, window=None, sink=None, diff=None,
               lam=None, subln=None, transposed=True, name="attn"):
    B, hq, n, dk = q.shape
    R = 2
    k_heads = ko.shape[1]
    v_heads = vo.shape[1]
    RK = R if k_heads == hq else 1
    RV = R if v_heads == hq else 1
    k_own = ko.shape[2]
    has_ctx = kc is not None
    cfg = dict(R=R, RK=RK, RV=RV, tq=tq, tk=min(tk, k_own), ko=k_own, scale=scale, has_ctx=has_ctx,
               window=window, sink=sink is not None, diff=diff, transposed=transposed)
    grid = (B, hq // R, n // tq)
    ins = [q]
    in_specs = [pl.BlockSpec((1, R, tq, dk), lambda b, g, i: (b, g, i, 0))]

    def kv_spec(arr, r, hmap):
        return pl.BlockSpec((1, r, arr.shape[2], arr.shape[3]), lambda b, g, i: (b, hmap(g), 0, 0))

    if has_ctx:
        ins += [kc, vc]
        in_specs += [kv_spec(kc, RK, kmap), kv_spec(vc, RV, vmap)]
    ins += [ko, vo]
    in_specs += [kv_spec(ko, RK, kmap), kv_spec(vo, RV, vmap)]
    if sink is not None:
        ins.append(sink)
        in_specs.append(pl.BlockSpec(memory_space=pltpu.SMEM))
    if diff is not None:
        ins += [lam, subln]
        in_specs += [_const_spec(lam.shape), _const_spec(subln.shape)]
    out_w = (hq // R) * LANES_V7X
    return pl.pallas_call(
        functools.partial(_attn_body, cfg),
        out_shape=jax.ShapeDtypeStruct((B, n, out_w), BF16),
        grid=grid, in_specs=in_specs,
        out_specs=pl.BlockSpec((1, tq, LANES_V7X), lambda b, g, i: (b, i, g)),
        compiler_params=_cparams(3), name=name,
    )(*ins)


def _merge_body(x_ref, mod_ref, g1_ref, g2_ref, oa_ref, ob_ref, oc_ref, od_ref, wg_ref, wb_ref, wo_ref,
                wrt_ref, x1_ref, h2_ref, aff3_ref):
    x = x_ref[0]
    m = mod_ref[0]
    D = D_MODEL
    hb = _ada_norm(x, g1_ref[...], m[:, 0:D], m[:, D:2 * D]).astype(BF16)
    merged = None
    for bi, o_ref in enumerate((oa_ref, ob_ref, oc_ref, od_ref)):
        gate = _sigmoid(jnp.dot(hb, wg_ref[:, bi * D:(bi + 1) * D], preferred_element_type=F32))
        br = jnp.dot(o_ref[0], wb_ref[bi], preferred_element_type=F32)
        merged = gate * br if merged is None else merged + gate * br
    mix = jnp.dot(merged.astype(BF16), wo_ref[...], preferred_element_type=F32)
    x1 = x + m[:, 2 * D:3 * D] * mix
    x1_ref[0] = x1
    h2 = _ada_norm(x1, g2_ref[...], m[:, 3 * D:4 * D], m[:, 4 * D:5 * D])
    h2b = h2.astype(BF16)
    h2_ref[0] = h2b
    lt = lax.dot_general(wrt_ref[...], h2b, (((1,), (1,)), ((), ())), preferred_element_type=F32)
    et = jnp.exp(lt - jnp.max(lt, axis=0, keepdims=True))
    at = et / jnp.sum(et, axis=0, keepdims=True)
    for ti in range(x.shape[0] // LANES_V7X):
        aff3_ref[0, ti] = at[:, ti * LANES_V7X:(ti + 1) * LANES_V7X]


def _merge(x, mod, P, l, outs, t_tile):
    B, n, D = x.shape
    nt = n // t_tile
    mod_b = mod.shape[0]
    tl = t_tile // LANES_V7X

    def tok_spec(w):
        return pl.BlockSpec((1, t_tile, w), lambda b, i: (b, i, 0))

    def w_spec(shape):
        nd = len(shape)
        return pl.BlockSpec(shape, lambda b, i: (0,) * nd, pipeline_mode=pl.Buffered(1))

    ins = [x, mod, P["g1"][l], P["g2"][l], *outs, P["wgates"][l], P["wbranch"][l], P["wout"][l],
           P["wrouter_t"][l]]
    in_specs = [
        tok_spec(D),
        pl.BlockSpec((1, 1, 6 * D), (lambda b, i: (b, 0, 0)) if mod_b > 1 else (lambda b, i: (0, 0, 0))),
        _const_spec((1, D)), _const_spec((1, D)),
        tok_spec(BRANCH_W), tok_spec(BRANCH_W), tok_spec(BRANCH_W), tok_spec(BRANCH_W),
        w_spec((D, N_BRANCH * D)), w_spec((N_BRANCH, BRANCH_W, D)), w_spec((D, D)),
        _const_spec((N_EXPERTS, D)),
    ]
    out_shape = [
        jax.ShapeDtypeStruct((B, n, D), F32),
        jax.ShapeDtypeStruct((B, n, D), BF16),
        jax.ShapeDtypeStruct((B, n // LANES_V7X, N_EXPERTS, LANES_V7X), F32),
    ]
    out_specs = [
        tok_spec(D), tok_spec(D),
        pl.BlockSpec((1, tl, N_EXPERTS, LANES_V7X), lambda b, i: (b, i, 0, 0)),
    ]
    return pl.pallas_call(
        _merge_body, out_shape=out_shape, grid=(B, nt), in_specs=in_specs, out_specs=out_specs,
        compiler_params=_cparams(2), name="merge",
    )(*ins)


def _route_body(cap, aff_ref, tri_ref, ones_ref, pos_ref, starts_ref, tot_sc, off_sc):
    nt = aff_ref.shape[0]
    E = N_EXPERTS
    bits = pltpu.bitcast(aff_ref[...], I32)

    def count_ge(th):
        c = jnp.sum(jnp.where(bits >= th[None], 1, 0), axis=0)
        return jnp.broadcast_to(jnp.sum(c, axis=1, keepdims=True), (E, LANES_V7X))

    def bisect(_, carry):
        lo, hi = carry
        mid = lo + ((hi - lo) >> 1)
        ok = count_ge(mid) >= cap
        return jnp.where(ok, mid, lo), jnp.where(ok, hi, mid)

    lo0 = jnp.zeros((E, LANES_V7X), I32)
    hi0 = jnp.full((E, LANES_V7X), 0x7F800000, I32)
    thr, _ = lax.fori_loop(0, 31, bisect, (lo0, hi0))

    gt = bits > thr[None]
    eq = bits == thr[None]
    gt_f = jnp.where(gt, 1.0, 0.0)
    eq_f = jnp.where(eq, 1.0, 0.0)
    n_gt = jnp.sum(jnp.sum(gt_f, axis=0), axis=1, keepdims=True)
    need = jnp.broadcast_to(float(cap) - n_gt, (E, LANES_V7X))[None]

    def prefix(mask_f):
        m2 = mask_f.astype(BF16).reshape(nt * E, LANES_V7X)
        incl = jnp.dot(m2, tri_ref[...], preferred_element_type=F32).reshape(nt, E, LANES_V7X)
        tot_sc[...] = jnp.dot(m2, ones_ref[...], preferred_element_type=F32).reshape(nt, E, LANES_V7X)

        def scan(i, run):
            off_sc[i] = run
            return run + tot_sc[i]
        lax.fori_loop(0, nt, scan, jnp.zeros((E, LANES_V7X), F32))
        offs = off_sc[...]
        return incl - mask_f + offs, offs

    ex_gt, off_gt = prefix(gt_f)
    ex_eq, off_eq = prefix(eq_f)
    sel = jnp.logical_or(gt, jnp.logical_and(eq, ex_eq < need))
    pos = ex_gt + jnp.minimum(ex_eq, need)
    pos_ref[...] = jnp.where(sel, pos.astype(I32), NO_SLOT)
    starts_ref[...] = (off_gt + jnp.minimum(off_eq, need)).astype(I32)


def _route(aff3, cap):
    nt = aff3.shape[0]
    j = np.arange(LANES_V7X)
    tri = jnp.asarray(j[:, None] <= j[None, :], BF16)
    ones = jnp.ones((LANES_V7X, LANES_V7X), BF16)
    shp = (nt, N_EXPERTS, LANES_V7X)
    return pl.pallas_call(
        functools.partial(_route_body, cap),
        out_shape=[jax.ShapeDtypeStruct(shp, I32), jax.ShapeDtypeStruct(shp, I32)],
        scratch_shapes=[pltpu.VMEM(shp, F32), pltpu.VMEM(shp, F32)],
        compiler_params=pltpu.CompilerParams(vmem_limit_bytes=VMEM_LIMIT_BYTES_V7X),
        name="route",
    )(aff3, tri, ones)


GATHER_CHUNK = 2 * LANES_V7X
GATHER_BUFS = 8


def _ffn_body(ts, kt, nt, starts_ref, pos_ref, h_hbm, wg_ref, wu_ref, wd_ref, o_ref, xe_ref, hbuf, sem, cur):
    k = pl.program_id(1)
    slot0 = k * ts
    @pl.when(k == 0)
    def _():
        cur[0] = 0

    i_lo = lax.while_loop(lambda i: starts_ref[0, 0, i + 1] <= slot0, lambda i: i + 1, cur[0])
    i_hi = lax.while_loop(lambda i: jnp.logical_and(i + 1 < nt, starts_ref[0, 0, i + 1] < slot0 + ts),
                          lambda i: i + 1, i_lo)
    cur[0] = i_lo
    tiles_per_chunk = GATHER_CHUNK // LANES_V7X
    c_lo = i_lo // tiles_per_chunk
    c_hi = i_hi // tiles_per_chunk

    def chunk_copy(c, b):
        return pltpu.make_async_copy(h_hbm.at[pl.ds(c * GATHER_CHUNK, GATHER_CHUNK), :], hbuf.at[b], sem.at[b])

    for d in range(GATHER_BUFS):
        @pl.when(c_lo + d <= c_hi)
        def _(d=d):
            chunk_copy(c_lo + d, d).start()

    xe_ref[...] = jnp.zeros(xe_ref.shape, BF16)
    gsub = min(LANES_V7X, ts)
    sub_iota = lax.broadcasted_iota(I32, (gsub, LANES_V7X), 0)

    def body(c, carry):
        b = (c - c_lo) % GATHER_BUFS
        chunk_copy(c, b).wait()

        s_lo = starts_ref[0, 0, c * tiles_per_chunk]
        s_hi = starts_ref[0, 0, (c + 1) * tiles_per_chunk]
        ps = [pos_ref[0, pl.ds(c * tiles_per_chunk + j, 1), :] for j in range(tiles_per_chunk)]
        for si in range(ts // gsub):
            lo = slot0 + si * gsub

            @pl.when(jnp.logical_and(s_lo < lo + gsub, s_hi > lo))
            def _(si=si, lo=lo):
                ids = lo + sub_iota
                onehot = jnp.concatenate([jnp.where(p == ids, 1.0, 0.0) for p in ps], axis=1).astype(BF16)
                rows = slice(si * gsub, (si + 1) * gsub)
                xe_ref[rows, :] += jnp.dot(onehot, hbuf[b], preferred_element_type=F32).astype(BF16)

        @pl.when(c + GATHER_BUFS <= c_hi)
        def _():
            chunk_copy(c + GATHER_BUFS, b).start()
        return carry

    lax.fori_loop(c_lo, c_hi + 1, body, 0)
    x = xe_ref[...]
    g = jnp.dot(x, wg_ref[0], preferred_element_type=F32)
    u = jnp.dot(x, wu_ref[0], preferred_element_type=F32)
    a = ((g * _sigmoid(g)) * u).astype(BF16)
    o_ref[...] = jnp.dot(a, wd_ref[0], preferred_element_type=F32).astype(BF16)


def _ffn(h2, pos_e, starts_e, wg, wu, wd, cap, ts):
    n_tok, D = h2.shape
    E = N_EXPERTS
    kt = cap // ts
    nt = pos_e.shape[1]
    F = D_EXPERT
    return pl.pallas_call(
        functools.partial(_ffn_body, ts, kt, nt),
        out_shape=jax.ShapeDtypeStruct((E * cap, D), BF16),
        grid=(E, kt),
        in_specs=[
            pl.BlockSpec((1, 1, nt + 1), lambda e, k: (e, 0, 0), memory_space=pltpu.SMEM),
            pl.BlockSpec((1, nt, LANES_V7X), lambda e, k: (e, 0, 0)),
            pl.BlockSpec(memory_space=pl.ANY),
            pl.BlockSpec((1, D, F), lambda e, k: (e, 0, 0)),
            pl.BlockSpec((1, D, F), lambda e, k: (e, 0, 0)),
            pl.BlockSpec((1, F, D), lambda e, k: (e, 0, 0)),
        ],
        out_specs=pl.BlockSpec((ts, D), lambda e, k: (e * kt + k, 0)),
        scratch_shapes=[pltpu.VMEM((ts, D), BF16), pltpu.VMEM((GATHER_BUFS, GATHER_CHUNK, D), BF16),
                        pltpu.SemaphoreType.DMA((GATHER_BUFS,)), pltpu.SMEM((1,), I32)],
        compiler_params=_cparams(2), name="ffn",
    )(starts_e, pos_e, h2, wg, wu, wd)


SLOT_CHUNK = LANES_V7X


def _combine_body(cap, st_ref, en_ref, x_ref, mod_ref, aff_ref, pos_ref, ye_hbm, o_ref, ybuf, sem, c_exp, c_base):
    E = N_EXPERTS
    tc = x_ref.shape[1]
    n_rows = E * cap
    sub = tc // LANES_V7X

    def chunk_copy(row0, idx):
        return pltpu.make_async_copy(ye_hbm.at[pl.ds(row0, SLOT_CHUNK), :], ybuf.at[idx], sem.at[0])

    def add_chunk(e, row0, base, idx):
        c_exp[idx] = e
        c_base[idx] = base
        chunk_copy(row0, idx).start()
        return idx + 1

    def issue_expert(e, idx):
        st, en = st_ref[0, 0, e], en_ref[0, 0, e]
        q0 = st // SLOT_CHUNK
        q1 = jnp.where(en > st, (en + SLOT_CHUNK - 1) // SLOT_CHUNK, q0)

        def one(q, idx):
            row0 = jnp.minimum(e * cap + q * SLOT_CHUNK, n_rows - SLOT_CHUNK)
            return add_chunk(e, row0, row0 - e * cap, idx)
        return lax.fori_loop(q0, q1, one, idx)

    total = lax.fori_loop(0, E, issue_expert, jnp.int32(0))
    total = lax.cond(total % 2 == 1, lambda t: add_chunk(0, 0, NO_SLOT // 2, t), lambda t: t, total)

    def wait_one(j, c):
        chunk_copy(0, j).wait()
        return c
    lax.fori_loop(0, total, wait_one, 0)

    slot_iota = lax.broadcasted_iota(I32, (SLOT_CHUNK, LANES_V7X), 0)

    def weighted_onehot_t(idx):
        e, base = c_exp[idx], c_base[idx]
        parts = []
        for j in range(sub):
            p = pos_ref[j, pl.ds(e, 1), :]
            w = aff_ref[j, pl.ds(e, 1), :]
            parts.append(jnp.where(p == base + slot_iota, w, 0.0))
        return jnp.concatenate(parts, axis=1)

    def pair(pi, acc):
        i0 = 2 * pi
        wt = jnp.concatenate([weighted_onehot_t(i0), weighted_onehot_t(i0 + 1)], axis=0).T
        hi = wt.astype(BF16)
        lo = (wt - hi.astype(F32)).astype(BF16)
        y = jnp.concatenate([ybuf[i0], ybuf[i0 + 1]], axis=0)
        return acc + jnp.dot(hi, y, preferred_element_type=F32) + jnp.dot(lo, y, preferred_element_type=F32)

    acc = lax.fori_loop(0, total // 2, pair, jnp.zeros((tc, D_MODEL), F32))
    m = mod_ref[0]
    o_ref[0] = x_ref[0] + m[:, 5 * D_MODEL:6 * D_MODEL] * acc


def _combine(x1, mod, aff3, ye, pos3, st_t, en_t, cap, tc):
    B, n, D = x1.shape
    nt = n // tc
    sub = tc // LANES_V7X
    mod_b = mod.shape[0]
    E = N_EXPERTS
    max_chunks = E * (tc // SLOT_CHUNK + 1) + 2
    return pl.pallas_call(
        functools.partial(_combine_body, cap),
        out_shape=jax.ShapeDtypeStruct((B, n, D), F32),
        grid=(B, nt),
        in_specs=[
            pl.BlockSpec((1, 1, E), lambda b, i: (b * nt + i, 0, 0), memory_space=pltpu.SMEM),
            pl.BlockSpec((1, 1, E), lambda b, i: (b * nt + i, 0, 0), memory_space=pltpu.SMEM),
            pl.BlockSpec((1, tc, D), lambda b, i: (b, i, 0)),
            pl.BlockSpec((1, 1, 6 * D), (lambda b, i: (b, 0, 0)) if mod_b > 1 else (lambda b, i: (0, 0, 0))),
            pl.BlockSpec((sub, E, LANES_V7X), lambda b, i: (b * nt + i, 0, 0)),
            pl.BlockSpec((sub, E, LANES_V7X), lambda b, i: (b * nt + i, 0, 0)),
            pl.BlockSpec(memory_space=pl.ANY),
        ],
        out_specs=pl.BlockSpec((1, tc, D), lambda b, i: (b, i, 0)),
        scratch_shapes=[pltpu.VMEM((max_chunks, SLOT_CHUNK, D), BF16), pltpu.SemaphoreType.DMA((1,)),
                        pltpu.SMEM((max_chunks,), I32), pltpu.SMEM((max_chunks,), I32)],
        compiler_params=_cparams(2), name="combine",
    )(st_t, en_t, x1, mod, aff3, pos3, ye)


def _expert_choice_ffn(x1, h2, aff3, mod, P, l):
    B, n, D = x1.shape
    E = N_EXPERTS
    n_tok = B * n
    cap = EC_FACTOR * n_tok // E
    nt = n_tok // LANES_V7X
    aff3 = aff3.reshape(nt, E, LANES_V7X)
    pos3, starts3 = _route(aff3, cap)
    starts = starts3[:, :, 0]
    pos_e = jnp.transpose(pos3, (1, 0, 2))
    starts_e = jnp.concatenate([starts.T, jnp.full((E, 1), cap, I32)], axis=1).reshape(E, 1, nt + 1)
    ts = min(512, cap)
    ye = _ffn(h2.reshape(n_tok, D), pos_e, starts_e, P["wgate"][l], P["wup"][l], P["wdown"][l], cap, ts)
    tc = min(512, n)
    sub = tc // LANES_V7X
    st_t = starts[::sub]
    en_t = jnp.concatenate([starts[sub::sub], jnp.full((1, E), cap, I32)], axis=0)
    ntc = n_tok // tc
    return _combine(x1, mod, aff3, ye, pos3, st_t.reshape(ntc, 1, E), en_t.reshape(ntc, 1, E), cap, tc)


def _layer(x, mod, P, l, ctx, rope_tabs):
    B, n, _ = x.shape
    latent = ctx is not None
    t_tile = 256
    outs = _proj(x, mod, P, l, rope_tabs if latent else None, not latent, t_tile)
    qa, ka, va, qb, kb, vb, qc, kc, vc, qd, kd, vd = outs[:12]
    if latent:
        c_ak, c_av, c_bk, c_bv, c_kc, c_vc, c_dk, c_dv = ctx
        tq, tk = 512, 512
    else:
        c_ak = c_av = c_bk = c_bv = c_kc = c_vc = c_dk = c_dv = None
        tq, tk = n, n
    lam_init = _lambda_init(l)
    oa = _attention(qa, ka, va, c_ak, c_av, scale=SCALE_A, kmap=lambda g: g, vmap=lambda g: g, tq=tq, tk=tk,
                    diff=lam_init, lam=P["alam"][l], subln=P["asubln_col"][l], name="attn_a")
    ob = _attention(qb, kb, vb, c_bk, c_bv, scale=SCALE_B, kmap=lambda g: g // 2, vmap=lambda g: g // 2,
                    tq=tq, tk=tk, name="attn_b")
    oc = _attention(qc, kc, vc, c_kc, c_vc, scale=SCALE_C, kmap=lambda g: g, vmap=lambda g: g, tq=tq, tk=tk,
                    name="attn_c")
    od = _attention(qd, kd, vd, c_dk, c_dv, scale=SCALE_D, kmap=lambda g: g // 2, vmap=lambda g: g // 2,
                    tq=tq, tk=tk, window=WINDOW if latent else None, sink=P["dsink"][l], transposed=False,
                    name="attn_d")
    x1, h2, aff3 = _merge(x, mod, P, l, (oa, ob, oc, od), t_tile)
    x2 = _expert_choice_ffn(x1, h2, aff3, mod, P, l)
    return x2, outs[12:]


def _rope_tables(n, dim):
    rows = n // GRID_W
    row = jnp.repeat(jnp.arange(rows, dtype=F32), GRID_W)
    col = jnp.tile(jnp.arange(GRID_W, dtype=F32), rows)
    nf = dim // 4
    inv = ROPE_THETA ** (-jnp.arange(nf, dtype=F32) / nf)
    ar = row[:, None] * inv
    ac = col[:, None] * inv
    ang = jnp.concatenate([ar, ar, ac, ac], axis=-1)
    cos, sin = jnp.cos(ang), jnp.sin(ang)
    first = (jnp.arange(dim) % (2 * nf)) < nf
    sin_a = jnp.where(first[None, :], -sin, 0.0)
    sin_b = jnp.where(first[None, :], 0.0, sin)
    reps = LANES_V7X // dim
    return tuple(jnp.tile(t, (1, reps)) for t in (cos, sin_a, sin_b))


def _prepare_params(ada_w, ada_b, norm1_g, norm2_g, w_in, a_q_norm, a_k_norm, a_lambda, a_subln, b_q_norm,
                    b_k_norm, c_qa_norm, c_kva_norm, c_wq_up, c_wkv_up, c_q_norm, c_k_norm, c_qr_norm,
                    c_kr_norm, d_q_norm, d_k_norm, d_sink, w_branch, w_out, w_router, w_gate, w_up, w_down):
    L = w_in.shape[0]

    def row(a):
        return a[:, None, :].astype(F32)

    def tiled(a, reps):
        return jnp.tile(a, (1, reps))[:, None, :].astype(F32)

    w1 = jnp.concatenate([w_in[:, :, :_CKR_END_ORIG],
                          jnp.zeros((L, D_MODEL, _C_CKR[1] - _C_CKR[0] - ROPE_C), w_in.dtype),
                          w_in[:, :, _CKR_END_ORIG:_W_ORIG_NOGATE]], axis=-1).astype(BF16)
    wq = c_wq_up.reshape(L, Q_LORA, H_C, NOPE_C + ROPE_C)
    wqup = jnp.concatenate([wq[..., :NOPE_C].reshape(L, Q_LORA, H_C * NOPE_C),
                            wq[..., NOPE_C:].reshape(L, Q_LORA, H_C * ROPE_C)], axis=-1).astype(BF16)
    wkv = c_wkv_up.reshape(L, KV_LORA, H_C, NOPE_C + V_C)
    wkvup = jnp.concatenate([wkv[..., :NOPE_C].reshape(L, KV_LORA, H_C * NOPE_C),
                             wkv[..., NOPE_C:].reshape(L, KV_LORA, H_C * V_C)], axis=-1).astype(BF16)
    ckrn = jnp.concatenate([c_kr_norm, jnp.zeros((L, LANES_V7X - ROPE_C), c_kr_norm.dtype)], axis=-1)
    return dict(
        g1=row(norm1_g), g2=row(norm2_g), w1=w1,
        aqn=tiled(a_q_norm, 2 * H_A), akn=tiled(a_k_norm, 2 * H_A),
        bqn=tiled(b_q_norm, H_B), bkn=tiled(b_k_norm, KV_B),
        cqan=row(c_qa_norm), ckvan=row(c_kva_norm), wqup=wqup, wkvup=wkvup,
        cqn=tiled(c_q_norm, H_C), ckn=tiled(c_k_norm, H_C), cqrn=tiled(c_qr_norm, H_C), ckrn=row(ckrn),
        dqn=tiled(d_q_norm, H_D), dkn=tiled(d_k_norm, KV_D),
        alam=a_lambda.astype(F32), asubln_col=a_subln[:, :, None].astype(F32), dsink=d_sink.astype(F32),
        wgates=w_in[:, :, _W_ORIG_NOGATE:].astype(BF16), wbranch=w_branch.astype(BF16), wout=w_out.astype(BF16),
        wrouter_t=jnp.transpose(w_router, (0, 2, 1)).astype(BF16),
        wgate=w_gate.astype(BF16), wup=w_up.astype(BF16), wdown=w_down.astype(BF16),
    )


def _head_major(c):
    return jnp.transpose(c, (0, 2, 1, 3)).astype(BF16)


def _head_major_t(c):
    return jnp.transpose(c, (0, 2, 3, 1)).astype(BF16)


def kernel(x_prompt, x_sample, c, cache_a_k, cache_a_v, cache_b_k, cache_b_v, cache_c_kv, cache_c_kr, cache_d_k, cache_d_v, c_ctx, ada_w, ada_b, norm1_g, norm2_g, w_in, a_q_norm, a_k_norm, a_lambda, a_subln, b_q_norm, b_k_norm, c_qa_norm, c_kva_norm, c_wq_up, c_wkv_up, c_q_norm, c_k_norm, c_qr_norm, c_kr_norm, d_q_norm, d_k_norm, d_sink, w_branch, w_out, w_router, w_gate, w_up, w_down):
    L = w_in.shape[0]
    P = _prepare_params(ada_w, ada_b, norm1_g, norm2_g, w_in, a_q_norm, a_k_norm, a_lambda, a_subln, b_q_norm,
                        b_k_norm, c_qa_norm, c_kva_norm, c_wq_up, c_wkv_up, c_q_norm, c_k_norm, c_qr_norm,
                        c_kr_norm, d_q_norm, d_k_norm, d_sink, w_branch, w_out, w_router, w_gate, w_up, w_down)
    dec_b = c.shape[0]
    rows = 8 * ((1 + dec_b + 7) // 8)
    cond = jnp.concatenate([c_ctx[None, :], c, jnp.zeros((rows - 1 - dec_b, D_MODEL), F32)], axis=0)
    mod = _modulation(cond, ada_w, ada_b)
    mod_ctx = mod[:, 0:1, None, :]
    mod_lat = mod[:, 1:1 + dec_b, None, :]

    x = x_prompt
    caches = []
    for l in range(L):
        x, cache = _layer(x, mod_ctx[l], P, l, None, None)
        caches.append(cache)
    y_prompt = x
    B, n = x_prompt.shape[:2]
    new = [jnp.stack([lc[j] for lc in caches], axis=1) for j in range(8)]
    new_a_k = new[0].reshape(B, L, n, 2 * H_A, HD_A)
    new_a_v = new[1].reshape(B, L, n, H_A, 2 * HD_A)
    new_b_k = new[2].reshape(B, L, n, KV_B, HD_B)
    new_b_v = new[3].reshape(B, L, n, KV_B, HD_B)
    new_c_kv, new_c_kr = new[4], new[5]
    new_d_k = new[6].reshape(B, L, n, KV_D, HD_D)
    new_d_v = new[7].reshape(B, L, n, KV_D, HD_D)

    n_lat = x_sample.shape[1]
    rope_tabs = _rope_tables(n_lat, HD_A) + _rope_tables(n_lat, ROPE_C)
    x = x_sample
    for l in range(L):
        kc_ctx, vc_ctx = _mla_cache(cache_c_kv[:, l], cache_c_kr[:, l], P["wkvup"][l], P["ckn"][l])
        ctx = (_head_major(cache_a_k[:, l]), _head_major_t(cache_a_v[:, l]),
               _head_major(cache_b_k[:, l]), _head_major_t(cache_b_v[:, l]),
               kc_ctx, vc_ctx,
               _head_major(cache_d_k[:, l]), _head_major(cache_d_v[:, l]))
        x, _ = _layer(x, mod_lat[l], P, l, ctx, rope_tabs)
    y_sample = x
    return (y_prompt, y_sample, new_a_k, new_a_v, new_b_k, new_b_v, new_c_kv, new_c_kr, new_d_k, new_d_v)
```

```python
import functools
import math

import numpy as np
import jax
import jax.numpy as jnp
from jax import lax
from jax.experimental import pallas as pl
from jax.experimental.pallas import tpu as pltpu

F32 = jnp.float32
BF16 = jnp.bfloat16
I32 = jnp.int32

D_MODEL = 1024
GRID_W = 64
ROPE_THETA = 10000.0
EPS = 1e-6
NEG_BIG = -1e30
LOG2E = 1.4426950408889634

H_A, HD_A = 4, 64
H_B, KV_B, HD_B = 8, 2, 64
H_C, Q_LORA, KV_LORA, NOPE_C, ROPE_C, V_C = 8, 384, 256, 64, 32, 64
H_D, KV_D, HD_D = 8, 2, 64
WINDOW = 128
N_BRANCH, BRANCH_W = 4, 512
N_EXPERTS, D_EXPERT, EC_FACTOR = 16, 1408, 2

SCALE_A = HD_A ** -0.5
SCALE_B = HD_B ** -0.5
SCALE_C = (NOPE_C + ROPE_C) ** -0.5
SCALE_D = HD_D ** -0.5

LANES_V7X = 128
VMEM_LIMIT_BYTES_V7X = 56 * 1024 * 1024

_W_ORIG_NOGATE = 3744
_C_AQ, _C_AK, _C_AV = (0, 512), (512, 1024), (1024, 1536)
_C_BQ, _C_BK, _C_BV = (1536, 2048), (2048, 2176), (2176, 2304)
_C_CQ, _C_CKV, _C_CKR = (2304, 2688), (2688, 2944), (2944, 3072)
_C_DQ, _C_DK, _C_DV = (3072, 3584), (3584, 3712), (3712, 3840)
_W1_COLS = 3840
_CKR_END_ORIG = 2976

NO_SLOT = -(2 ** 30)


def _lambda_init(l):
    return 0.8 - 0.6 * math.exp(-0.3 * l)


def _cparams(n_axes):
    return pltpu.CompilerParams(
        dimension_semantics=("arbitrary",) * n_axes,
        vmem_limit_bytes=VMEM_LIMIT_BYTES_V7X,
    )


def _const_spec(shape):
    nd = len(shape)
    return pl.BlockSpec(shape, lambda *_: (0,) * nd)


def _inv_rms(z):
    return lax.rsqrt(jnp.mean(z * z, axis=-1, keepdims=True) + EPS)


def _sigmoid(x):
    return 1.0 / (1.0 + jnp.exp(-x))


def _mod_body(c_ref, w_ref, b_ref, o_ref):
    c = c_ref[...]
    s = (c * _sigmoid(c)).astype(BF16)
    o_ref[0] = jnp.dot(s, w_ref[0].astype(BF16), preferred_element_type=F32) + b_ref[0]


def _modulation(cond, ada_w, ada_b):
    L = ada_w.shape[0]
    R = cond.shape[0]
    nblk = 6
    return pl.pallas_call(
        _mod_body,
        out_shape=jax.ShapeDtypeStruct((L, R, 6 * D_MODEL), F32),
        grid=(L, nblk),
        in_specs=[
            pl.BlockSpec((R, D_MODEL), lambda l, j: (0, 0)),
            pl.BlockSpec((1, D_MODEL, D_MODEL), lambda l, j: (l, 0, j)),
            pl.BlockSpec((1, 1, D_MODEL), lambda l, j: (l, 0, j)),
        ],
        out_specs=pl.BlockSpec((1, R, D_MODEL), lambda l, j: (l, 0, j)),
        compiler_params=_cparams(2),
        name="modulation",
    )(cond, ada_w, ada_b.reshape(L, 1, 6 * D_MODEL))


def _ada_norm(x, g, shift, scale):
    return (x * _inv_rms(x) * g) * (1.0 + scale) + shift


def _rotary(zg, cos, sin_a, sin_b, nf):
    w = zg.shape[-1]
    return zg * cos + pltpu.roll(zg, w - nf, 1) * sin_a + pltpu.roll(zg, nf, 1) * sin_b


def _tile_lanes(t, width):
    reps = width // t.shape[-1]
    return t if reps == 1 else jnp.concatenate([t] * reps, axis=1)


def _head_norm(z, g_tiled, n_heads, hd, rope):
    zg = z * g_tiled
    w = z.shape[-1]
    if rope is not None:
        cos, sin_a, sin_b, nf = rope
        zg = _rotary(zg, _tile_lanes(cos, w), _tile_lanes(sin_a, w), _tile_lanes(sin_b, w), nf)
    ind = jnp.where(lax.broadcasted_iota(I32, (w, LANES_V7X), 0) // hd
                    == lax.broadcasted_iota(I32, (w, LANES_V7X), 1), 1.0, 0.0).astype(BF16)
    ind_t = jnp.where(lax.broadcasted_iota(I32, (LANES_V7X, w), 1) // hd
                      == lax.broadcasted_iota(I32, (LANES_V7X, w), 0), 1.0, 0.0).astype(BF16)

    def hi_lo_dot(a, b):
        hi = a.astype(BF16)
        lo = (a - hi.astype(F32)).astype(BF16)
        return jnp.dot(hi, b, preferred_element_type=F32) + jnp.dot(lo, b, preferred_element_type=F32)

    inv = lax.rsqrt(hi_lo_dot(z * z, ind) * (1.0 / hd) + EPS)
    y = zg * hi_lo_dot(inv, ind_t)
    return [y[:, h * hd:(h + 1) * hd] for h in range(n_heads)]


def _mla_expand(ckv_n, kr, wkv_ref, ckn_tiled, kc_ref, vc_ref):
    kv = jnp.dot(ckv_n.astype(BF16), wkv_ref[...], preferred_element_type=F32)
    kn = _head_norm(kv[:, :H_C * NOPE_C], ckn_tiled, H_C, NOPE_C, None)
    krb = kr.astype(BF16)
    t = kv.shape[0]
    pad = jnp.zeros((t, LANES_V7X - NOPE_C - ROPE_C), BF16)
    for h in range(H_C):
        kc_ref[0, h, :, 0:NOPE_C] = kn[h].astype(BF16)
        kc_ref[0, h, :, NOPE_C:NOPE_C + ROPE_C] = krb
        kc_ref[0, h, :, NOPE_C + ROPE_C:LANES_V7X] = pad
    vt = kv[:, H_C * NOPE_C:].astype(BF16).T
    for h in range(H_C):
        vc_ref[0, h] = vt[h * V_C:(h + 1) * V_C, :]


def _proj_body(rope, emit_cache, *refs):
    it = iter(refs)
    x_ref, mod_ref, g1_ref, w1_ref = next(it), next(it), next(it), next(it)
    aqn, akn, bqn, bkn = next(it), next(it), next(it), next(it)
    cqan, ckvan, wqup_ref, wkvup_ref = next(it), next(it), next(it), next(it)
    cqn, ckn, cqrn, ckrn, dqn, dkn = next(it), next(it), next(it), next(it), next(it), next(it)
    if rope:
        cos64, sa64, sb64, cos32, sa32, sb32 = (next(it)[...] for _ in range(6))
        rope64 = (cos64, sa64, sb64, HD_A // 4)
        rope32 = (cos32, sa32, sb32, ROPE_C // 4)
    else:
        rope64 = rope32 = None
    qa_ref, ka_ref, va_ref = next(it), next(it), next(it)
    qb_ref, kb_ref, vb_ref = next(it), next(it), next(it)
    qc_ref, kc_ref, vc_ref = next(it), next(it), next(it)
    qd_ref, kd_ref, vd_ref = next(it), next(it), next(it)
    if emit_cache:
        c_ak, c_av, c_bk, c_bv, c_ckv, c_ckr, c_dk, c_dv = (next(it) for _ in range(8))

    x = x_ref[0]
    m = mod_ref[0]
    hb = _ada_norm(x, g1_ref[...], m[:, 0:D_MODEL], m[:, D_MODEL:2 * D_MODEL]).astype(BF16)
    t = x.shape[0]

    def seg(c):
        return jnp.dot(hb, w1_ref[:, c[0]:c[1]], preferred_element_type=F32)

    def store_heads(ys, ref):
        for h, y in enumerate(ys):
            ref[0, h] = y.astype(BF16)

    def store_cache(ys, ref, hd):
        for h, y in enumerate(ys):
            ref[0, :, h * hd:(h + 1) * hd] = y

    store_heads(_head_norm(seg(_C_AQ), aqn[...], 2 * H_A, HD_A, rope64), qa_ref)
    ys = _head_norm(seg(_C_AK), akn[...], 2 * H_A, HD_A, rope64)
    store_heads(ys, ka_ref)
    if emit_cache:
        store_cache(ys, c_ak, HD_A)
    av = seg(_C_AV)
    avt = av.astype(BF16).T
    for h in range(H_A):
        va_ref[0, h] = avt[h * 2 * HD_A:(h + 1) * 2 * HD_A, :]
    if emit_cache:
        c_av[0] = av

    for (cq, ck, cv, qn, kn_, q_ref, k_ref, v_ref, ck_ref, cv_ref) in (
        (_C_BQ, _C_BK, _C_BV, bqn, bkn, qb_ref, kb_ref, vb_ref,
         c_bk if emit_cache else None, c_bv if emit_cache else None),
        (_C_DQ, _C_DK, _C_DV, dqn, dkn, qd_ref, kd_ref, vd_ref,
         c_dk if emit_cache else None, c_dv if emit_cache else None),
    ):
        store_heads(_head_norm(seg(cq), qn[...], H_B, HD_B, rope64), q_ref)
        ys = _head_norm(seg(ck), kn_[...], KV_B, HD_B, rope64)
        store_heads(ys, k_ref)
        v = seg(cv)
        if v_ref is vb_ref:
            vt = v.astype(BF16).T
            for h in range(KV_B):
                v_ref[0, h] = vt[h * HD_B:(h + 1) * HD_B, :]
        else:
            for h in range(KV_B):
                v_ref[0, h] = v[:, h * HD_B:(h + 1) * HD_B].astype(BF16)
        if emit_cache:
            store_cache(ys, ck_ref, HD_B)
            cv_ref[0] = v

    cq = seg(_C_CQ)
    cq_n = (cq * _inv_rms(cq) * cqan[...]).astype(BF16)
    zq = jnp.dot(cq_n, wqup_ref[...], preferred_element_type=F32)
    q_nope = _head_norm(zq[:, :H_C * NOPE_C], cqn[...], H_C, NOPE_C, None)
    q_rope = _head_norm(zq[:, H_C * NOPE_C:], cqrn[...], H_C, ROPE_C, rope32)
    pad = jnp.zeros((t, LANES_V7X - NOPE_C - ROPE_C), BF16)
    for h in range(H_C):
        qc_ref[0, h, :, 0:NOPE_C] = q_nope[h].astype(BF16)
        qc_ref[0, h, :, NOPE_C:NOPE_C + ROPE_C] = q_rope[h].astype(BF16)
        qc_ref[0, h, :, NOPE_C + ROPE_C:LANES_V7X] = pad

    ckv = seg(_C_CKV)
    ckv_n = ckv * _inv_rms(ckv) * ckvan[...]
    ckr = seg(_C_CKR)
    kr_plain = ckr * ckrn[...] * _inv_rms(ckr[:, :ROPE_C])
    if emit_cache:
        c_ckv[0] = ckv_n
        c_ckr[0] = kr_plain[:, :ROPE_C]
    if rope:
        zg = ckr * ckrn[...]
        kr = _rotary(zg, cos32, sa32, sb32, ROPE_C // 4) * _inv_rms(ckr[:, :ROPE_C])
    else:
        kr = kr_plain
    _mla_expand(ckv_n, kr[:, :ROPE_C], wkvup_ref, ckn[...], kc_ref, vc_ref)


def _proj(x, mod, P, l, rope_tabs, emit_cache, t_tile):
    B, n, _ = x.shape
    nt = n // t_tile
    mod_b = mod.shape[0]
    grid = (B, nt)

    def tok_spec(w):
        return pl.BlockSpec((1, t_tile, w), lambda b, i: (b, i, 0))

    def head_spec(h, d):
        return pl.BlockSpec((1, h, t_tile, d), lambda b, i: (b, 0, i, 0))

    small = [P["aqn"][l], P["akn"][l], P["bqn"][l], P["bkn"][l], P["cqan"][l], P["ckvan"][l],
             P["wqup"][l], P["wkvup"][l], P["cqn"][l], P["ckn"][l], P["cqrn"][l], P["ckrn"][l],
             P["dqn"][l], P["dkn"][l]]
    ins = [x, mod, P["g1"][l], P["w1"][l]] + small
    in_specs = [
        tok_spec(D_MODEL),
        pl.BlockSpec((1, 1, 6 * D_MODEL), (lambda b, i: (b, 0, 0)) if mod_b > 1 else (lambda b, i: (0, 0, 0))),
        _const_spec((1, D_MODEL)),
        pl.BlockSpec((D_MODEL, _W1_COLS), lambda b, i: (0, 0), pipeline_mode=pl.Buffered(1)),
    ] + [_const_spec(a.shape) for a in small]
    rope = rope_tabs is not None
    if rope:
        ins += list(rope_tabs)
        in_specs += [pl.BlockSpec((t_tile, LANES_V7X), lambda b, i: (i, 0)) for _ in rope_tabs]

    def hs(h, d):
        return jax.ShapeDtypeStruct((B, h, n, d), BF16)

    def hst(h, d):
        return jax.ShapeDtypeStruct((B, h, d, n), BF16)

    def head_spec_t(h, d):
        return pl.BlockSpec((1, h, d, t_tile), lambda b, i: (b, 0, 0, i))

    out_shape = [hs(8, 64), hs(8, 64), hst(4, 128), hs(8, 64), hs(2, 64), hst(2, 64),
                 hs(8, 128), hs(8, 128), hst(8, 64), hs(8, 64), hs(2, 64), hs(2, 64)]
    out_specs = [head_spec(8, 64), head_spec(8, 64), head_spec_t(4, 128), head_spec(8, 64), head_spec(2, 64),
                 head_spec_t(2, 64), head_spec(8, 128), head_spec(8, 128), head_spec_t(8, 64), head_spec(8, 64),
                 head_spec(2, 64), head_spec(2, 64)]
    if emit_cache:
        for w in (512, 512, 128, 128, KV_LORA, ROPE_C, 128, 128):
            out_shape.append(jax.ShapeDtypeStruct((B, n, w), F32))
            out_specs.append(tok_spec(w))
    return pl.pallas_call(
        functools.partial(_proj_body, rope, emit_cache),
        out_shape=out_shape, grid=grid, in_specs=in_specs, out_specs=out_specs,
        compiler_params=_cparams(2), name="proj",
    )(*ins)


def _mla_cache_body(ckv_ref, ckr_ref, wkvup_ref, ckn_ref, kc_ref, vc_ref):
    _mla_expand(ckv_ref[0], ckr_ref[0], wkvup_ref, ckn_ref[...], kc_ref, vc_ref)


def _mla_cache(ckv, ckr, wkvup, ckn_tiled):
    B, K, _ = ckv.shape
    return pl.pallas_call(
        _mla_cache_body,
        out_shape=[jax.ShapeDtypeStruct((B, H_C, K, LANES_V7X), BF16),
                   jax.ShapeDtypeStruct((B, H_C, V_C, K), BF16)],
        grid=(B,),
        in_specs=[pl.BlockSpec((1, K, KV_LORA), lambda b: (b, 0, 0)),
                  pl.BlockSpec((1, K, ROPE_C), lambda b: (b, 0, 0)),
                  _const_spec(wkvup.shape), _const_spec(ckn_tiled.shape)],
        out_specs=[pl.BlockSpec((1, H_C, K, LANES_V7X), lambda b: (b, 0, 0, 0)),
                   pl.BlockSpec((1, H_C, V_C, K), lambda b: (b, 0, 0, 0))],
        compiler_params=_cparams(1), name="mla_cache",
    )(ckv, ckr, wkvup, ckn_tiled)


def _attn_body(cfg, *refs):
    it = iter(refs)
    q_ref = next(it)
    kc_ref = vc_ref = None
    if cfg["has_ctx"]:
        kc_ref, vc_ref = next(it), next(it)
    ko_ref, vo_ref = next(it), next(it)
    sink_ref = next(it) if cfg["sink"] else None
    if cfg["diff"] is not None:
        lam_ref, subln_ref = next(it), next(it)
    o_ref = next(it)

    R, RK, RV = cfg["R"], cfg["RK"], cfg["RV"]
    tq, tk, ko = cfg["tq"], cfg["tk"], cfg["ko"]
    c = cfg["scale"] * LOG2E
    dv = vo_ref.shape[-2] if cfg["transposed"] else vo_ref.shape[-1]
    g = pl.program_id(1)
    i = pl.program_id(2)
    nt_dims = (((1,), (1,)), ((), ()))
    rks = [r if RK > 1 else 0 for r in range(R)]
    rvs = [r if RV > 1 else 0 for r in range(R)]
    qs = [(q_ref[0, r].astype(F32) * c).astype(BF16) for r in range(R)]

    tr = cfg["transposed"]
    red = 0 if tr else -1

    def scores(q, k):
        if tr:
            return lax.dot_general(k, q, nt_dims, preferred_element_type=F32)
        return lax.dot_general(q, k, nt_dims, preferred_element_type=F32)

    def update(s, v, carry, mask):
        m, l, acc = carry
        if mask is not None:
            s = jnp.where(mask, s, NEG_BIG)
        m_new = jnp.maximum(m, jnp.max(s, axis=red, keepdims=True))
        alpha = jnp.exp2(m - m_new)
        p = jnp.exp2(s - m_new)
        l = alpha * l + jnp.sum(p, axis=red, keepdims=True)
        if tr:
            acc = alpha * acc + jnp.dot(v, p.astype(BF16), preferred_element_type=F32)
        else:
            acc = alpha * acc + jnp.dot(p.astype(BF16), v, preferred_element_type=F32)
        return m_new, l, acc

    def kv(k_ref, v_ref, r, st, w):
        if st is None:
            return k_ref[0, rks[r]], v_ref[0, rvs[r]]
        if tr:
            return k_ref[0, rks[r], pl.ds(st, w), :], v_ref[0, rvs[r], :, pl.ds(st, w)]
        return k_ref[0, rks[r], pl.ds(st, w), :], v_ref[0, rvs[r], pl.ds(st, w), :]

    def run_chunks(chunks, carries):
        def issue(ch):
            k_ref, v_ref, st, w, mask = ch
            kvs = [kv(k_ref, v_ref, r, st, w) for r in range(R)]
            return [scores(qs[r], kvs[r][0]) for r in range(R)], [x[1] for x in kvs], mask
        nxt = issue(chunks[0])
        for ci in range(len(chunks)):
            ss, vs, mask = nxt
            if ci + 1 < len(chunks):
                nxt = issue(chunks[ci + 1])
            carries = tuple(update(ss[r], vs[r], carries[r], mask) for r in range(R))
        return carries

    stat = (1, tq) if tr else (tq, 1)
    carries = []
    for r in range(R):
        if cfg["sink"]:
            m0 = jnp.full(stat, sink_ref[g * R + r] * LOG2E, F32)
            l0 = jnp.ones(stat, F32)
        else:
            m0 = jnp.full(stat, NEG_BIG, F32)
            l0 = jnp.zeros(stat, F32)
        carries.append((m0, l0, jnp.zeros((dv, tq) if tr else (tq, dv), F32)))
    chunks = []
    if cfg["has_ctx"]:
        chunks.append((kc_ref, vc_ref, None, None, None))
    if cfg["window"] is not None:
        win = cfg["window"]
        wk = min(tq + 2 * win, ko)
        q0 = i * tq
        k0 = pl.multiple_of(jnp.clip(q0 - win, 0, ko - wk), win)
        qpos = q0 + lax.broadcasted_iota(I32, (tq, wk), 0)
        kpos = k0 + lax.broadcasted_iota(I32, (tq, wk), 1)
        chunks.append((ko_ref, vo_ref, k0, wk, jnp.abs(qpos - kpos) <= win))
    elif ko == tk:
        chunks.append((ko_ref, vo_ref, None, None, None))
    else:
        for j in range(ko // tk):
            chunks.append((ko_ref, vo_ref, j * tk, tk, None))
    carries = run_chunks(chunks, tuple(carries))
    outs = [acc / l for (_, l, acc) in carries]

    if cfg["diff"] is not None:
        lp = lam_ref[...]
        lam = (jnp.exp(jnp.sum(lp[0:1] * lp[1:2], axis=-1, keepdims=True))
               - jnp.exp(jnp.sum(lp[2:3] * lp[3:4], axis=-1, keepdims=True)) + cfg["diff"])
        d = outs[0] - lam * outs[1]
        if tr:
            inv = lax.rsqrt(jnp.mean(d * d, axis=0, keepdims=True) + EPS)
            o_ref[0] = (d * inv * subln_ref[...] * (1.0 - cfg["diff"])).T.astype(BF16)
        else:
            o_ref[0] = (d * _inv_rms(d) * subln_ref[...] * (1.0 - cfg["diff"])).astype(BF16)
    elif tr:
        o_ref[0] = jnp.concatenate(outs, axis=0).T.astype(BF16)
    else:
        for r in range(R):
            o_ref[0, :, r * dv:(r + 1) * dv] = outs[r].astype(BF16)


def _attention(q, ko, vo, kc, vc, *, scale, kmap, vmap, tq, tk, window=None, sink=None, diff=None,
               lam=None, subln=None, transposed=True, name="attn"):
    B, hq, n, dk = q.shape
    R = 2
    k_heads = ko.shape[1]
    v_heads = vo.shape[1]
    RK = R if k_heads == hq else 1
    RV = R if v_heads == hq else 1
    k_own = ko.shape[2]
    has_ctx = kc is not None
    cfg = dict(R=R, RK=RK, RV=RV, tq=tq, tk=min(tk, k_own), ko=k_own, scale=scale, has_ctx=has_ctx,
               window=window, sink=sink is not None, diff=diff, transposed=transposed)
    grid = (B, hq // R, n // tq)
    ins = [q]
    in_specs = [pl.BlockSpec((1, R, tq, dk), lambda b, g, i: (b, g, i, 0))]

    def kv_spec(arr, r, hmap):
        return pl.BlockSpec((1, r, arr.shape[2], arr.shape[3]), lambda b, g, i: (b, hmap(g), 0, 0))

    if has_ctx:
        ins += [kc, vc]
        in_specs += [kv_spec(kc, RK, kmap), kv_spec(vc, RV, vmap)]
    ins += [ko, vo]
    in_specs += [kv_spec(ko, RK, kmap), kv_spec(vo, RV, vmap)]
    if sink is not None:
        ins.append(sink)
        in_specs.append(pl.BlockSpec(memory_space=pltpu.SMEM))
    if diff is not None:
        ins += [lam, subln]
        in_specs += [_const_spec(lam.shape), _const_spec(subln.shape)]
    out_w = (hq // R) * LANES_V7X
    return pl.pallas_call(
        functools.partial(_attn_body, cfg),
        out_shape=jax.ShapeDtypeStruct((B, n, out_w), BF16),
        grid=grid, in_specs=in_specs,
        out_specs=pl.BlockSpec((1, tq, LANES_V7X), lambda b, g, i: (b, i, g)),
        compiler_params=_cparams(3), name=name,
    )(*ins)


def _merge_body(x_ref, mod_ref, g1_ref, g2_ref, oa_ref, ob_ref, oc_ref, od_ref, wg_ref, wb_ref, wo_ref,
                wrt_ref, x1_ref, h2_ref, aff3_ref):
    x = x_ref[0]
    m = mod_ref[0]
    D = D_MODEL
    hb = _ada_norm(x, g1_ref[...], m[:, 0:D], m[:, D:2 * D]).astype(BF16)
    merged = None
    for bi, o_ref in enumerate((oa_ref, ob_ref, oc_ref, od_ref)):
        gate = _sigmoid(jnp.dot(hb, wg_ref[:, bi * D:(bi + 1) * D], preferred_element_type=F32))
        br = jnp.dot(o_ref[0], wb_ref[bi], preferred_element_type=F32)
        merged = gate * br if merged is None else merged + gate * br
    mix = jnp.dot(merged.astype(BF16), wo_ref[...], preferred_element_type=F32)
    x1 = x + m[:, 2 * D:3 * D] * mix
    x1_ref[0] = x1
    h2 = _ada_norm(x1, g2_ref[...], m[:, 3 * D:4 * D], m[:, 4 * D:5 * D])
    h2b = h2.astype(BF16)
    h2_ref[0] = h2b
    lt = lax.dot_general(wrt_ref[...], h2b, (((1,), (1,)), ((), ())), preferred_element_type=F32)
    et = jnp.exp(lt - jnp.max(lt, axis=0, keepdims=True))
    at = et / jnp.sum(et, axis=0, keepdims=True)
    for ti in range(x.shape[0] // LANES_V7X):
        aff3_ref[0, ti] = at[:, ti * LANES_V7X:(ti + 1) * LANES_V7X]


def _merge(x, mod, P, l, outs, t_tile):
    B, n, D = x.shape
    nt = n // t_tile
    mod_b = mod.shape[0]
    tl = t_tile // LANES_V7X

    def tok_spec(w):
        return pl.BlockSpec((1, t_tile, w), lambda b, i: (b, i, 0))

    def w_spec(shape):
        nd = len(shape)
        return pl.BlockSpec(shape, lambda b, i: (0,) * nd, pipeline_mode=pl.Buffered(1))

    ins = [x, mod, P["g1"][l], P["g2"][l], *outs, P["wgates"][l], P["wbranch"][l], P["wout"][l],
           P["wrouter_t"][l]]
    in_specs = [
        tok_spec(D),
        pl.BlockSpec((1, 1, 6 * D), (lambda b, i: (b, 0, 0)) if mod_b > 1 else (lambda b, i: (0, 0, 0))),
        _const_spec((1, D)), _const_spec((1, D)),
        tok_spec(BRANCH_W), tok_spec(BRANCH_W), tok_spec(BRANCH_W), tok_spec(BRANCH_W),
        w_spec((D, N_BRANCH * D)), w_spec((N_BRANCH, BRANCH_W, D)), w_spec((D, D)),
        _const_spec((N_EXPERTS, D)),
    ]
    out_shape = [
        jax.ShapeDtypeStruct((B, n, D), F32),
        jax.ShapeDtypeStruct((B, n, D), BF16),
        jax.ShapeDtypeStruct((B, n // LANES_V7X, N_EXPERTS, LANES_V7X), F32),
    ]
    out_specs = [
        tok_spec(D), tok_spec(D),
        pl.BlockSpec((1, tl, N_EXPERTS, LANES_V7X), lambda b, i: (b, i, 0, 0)),
    ]
    return pl.pallas_call(
        _merge_body, out_shape=out_shape, grid=(B, nt), in_specs=in_specs, out_specs=out_specs,
        compiler_params=_cparams(2), name="merge",
    )(*ins)


def _route_body(cap, aff_ref, tri_ref, ones_ref, pos_ref, starts_ref, tot_sc, off_sc):
    nt = aff_ref.shape[0]
    E = N_EXPERTS
    bits = pltpu.bitcast(aff_ref[...], I32)

    def count_ge(th):
        c = jnp.sum(jnp.where(bits >= th[None], 1, 0), axis=0)
        return jnp.broadcast_to(jnp.sum(c, axis=1, keepdims=True), (E, LANES_V7X))

    def bisect(_, carry):
        lo, hi = carry
        mid = lo + ((hi - lo) >> 1)
        ok = count_ge(mid) >= cap
        return jnp.where(ok, mid, lo), jnp.where(ok, hi, mid)

    lo0 = jnp.zeros((E, LANES_V7X), I32)
    hi0 = jnp.full((E, LANES_V7X), 0x7F800000, I32)
    thr, _ = lax.fori_loop(0, 31, bisect, (lo0, hi0))

    gt = bits > thr[None]
    eq = bits == thr[None]
    gt_f = jnp.where(gt, 1.0, 0.0)
    eq_f = jnp.where(eq, 1.0, 0.0)
    n_gt = jnp.sum(jnp.sum(gt_f, axis=0), axis=1, keepdims=True)
    need = jnp.broadcast_to(float(cap) - n_gt, (E, LANES_V7X))[None]

    def prefix(mask_f):
        m2 = mask_f.astype(BF16).reshape(nt * E, LANES_V7X)
        incl = jnp.dot(m2, tri_ref[...], preferred_element_type=F32).reshape(nt, E, LANES_V7X)
        tot_sc[...] = jnp.dot(m2, ones_ref[...], preferred_element_type=F32).reshape(nt, E, LANES_V7X)

        def scan(i, run):
            off_sc[i] = run
            return run + tot_sc[i]
        lax.fori_loop(0, nt, scan, jnp.zeros((E, LANES_V7X), F32))
        offs = off_sc[...]
        return incl - mask_f + offs, offs

    ex_gt, off_gt = prefix(gt_f)
    ex_eq, off_eq = prefix(eq_f)
    sel = jnp.logical_or(gt, jnp.logical_and(eq, ex_eq < need))
    pos = ex_gt + jnp.minimum(ex_eq, need)
    pos_ref[...] = jnp.where(sel, pos.astype(I32), NO_SLOT)
    starts_ref[...] = (off_gt + jnp.minimum(off_eq, need)).astype(I32)


def _route(aff3, cap):
    nt = aff3.shape[0]
    j = np.arange(LANES_V7X)
    tri = jnp.asarray(j[:, None] <= j[None, :], BF16)
    ones = jnp.ones((LANES_V7X, LANES_V7X), BF16)
    shp = (nt, N_EXPERTS, LANES_V7X)
    return pl.pallas_call(
        functools.partial(_route_body, cap),
        out_shape=[jax.ShapeDtypeStruct(shp, I32), jax.ShapeDtypeStruct(shp, I32)],
        scratch_shapes=[pltpu.VMEM(shp, F32), pltpu.VMEM(shp, F32)],
        compiler_params=pltpu.CompilerParams(vmem_limit_bytes=VMEM_LIMIT_BYTES_V7X),
        name="route",
    )(aff3, tri, ones)


GATHER_CHUNK = 2 * LANES_V7X
GATHER_BUFS = 8


def _ffn_body(ts, kt, nt, starts_ref, pos_ref, h_hbm, wg_ref, wu_ref, wd_ref, o_ref, xe_ref, hbuf, sem, cur):
    k = pl.program_id(1)
    slot0 = k * ts
    @pl.when(k == 0)
    def _():
        cur[0] = 0

    i_lo = lax.while_loop(lambda i: starts_ref[0, 0, i + 1] <= slot0, lambda i: i + 1, cur[0])
    i_hi = lax.while_loop(lambda i: jnp.logical_and(i + 1 < nt, starts_ref[0, 0, i + 1] < slot0 + ts),
                          lambda i: i + 1, i_lo)
    cur[0] = i_lo
    tiles_per_chunk = GATHER_CHUNK // LANES_V7X
    c_lo = i_lo // tiles_per_chunk
    c_hi = i_hi // tiles_per_chunk

    def chunk_copy(c, b):
        return pltpu.make_async_copy(h_hbm.at[pl.ds(c * GATHER_CHUNK, GATHER_CHUNK), :], hbuf.at[b], sem.at[b])

    for d in range(GATHER_BUFS):
        @pl.when(c_lo + d <= c_hi)
        def _(d=d):
            chunk_copy(c_lo + d, d).start()

    xe_ref[...] = jnp.zeros(xe_ref.shape, BF16)
    gsub = min(LANES_V7X, ts)
    sub_iota = lax.broadcasted_iota(I32, (gsub, LANES_V7X), 0)

    def body(c, carry):
        b = (c - c_lo) % GATHER_BUFS
        chunk_copy(c, b).wait()

        s_lo = starts_ref[0, 0, c * tiles_per_chunk]
        s_hi = starts_ref[0, 0, (c + 1) * tiles_per_chunk]
        ps = [pos_ref[0, pl.ds(c * tiles_per_chunk + j, 1), :] for j in range(tiles_per_chunk)]
        for si in range(ts // gsub):
            lo = slot0 + si * gsub

            @pl.when(jnp.logical_and(s_lo < lo + gsub, s_hi > lo))
            def _(si=si, lo=lo):
                ids = lo + sub_iota
                onehot = jnp.concatenate([jnp.where(p == ids, 1.0, 0.0) for p in ps], axis=1).astype(BF16)
                rows = slice(si * gsub, (si + 1) * gsub)
                xe_ref[rows, :] += jnp.dot(onehot, hbuf[b], preferred_element_type=F32).astype(BF16)

        @pl.when(c + GATHER_BUFS <= c_hi)
        def _():
            chunk_copy(c + GATHER_BUFS, b).start()
        return carry

    lax.fori_loop(c_lo, c_hi + 1, body, 0)
    x = xe_ref[...]
    g = jnp.dot(x, wg_ref[0], preferred_element_type=F32)
    u = jnp.dot(x, wu_ref[0], preferred_element_type=F32)
    a = ((g * _sigmoid(g)) * u).astype(BF16)
    o_ref[...] = jnp.dot(a, wd_ref[0], preferred_element_type=F32).astype(BF16)


def _ffn(h2, pos_e, starts_e, wg, wu, wd, cap, ts):
    n_tok, D = h2.shape
    E = N_EXPERTS
    kt = cap // ts
    nt = pos_e.shape[1]
    F = D_EXPERT
    return pl.pallas_call(
        functools.partial(_ffn_body, ts, kt, nt),
        out_shape=jax.ShapeDtypeStruct((E * cap, D), BF16),
        grid=(E, kt),
        in_specs=[
            pl.BlockSpec((1, 1, nt + 1), lambda e, k: (e, 0, 0), memory_space=pltpu.SMEM),
            pl.BlockSpec((1, nt, LANES_V7X), lambda e, k: (e, 0, 0)),
            pl.BlockSpec(memory_space=pl.ANY),
            pl.BlockSpec((1, D, F), lambda e, k: (e, 0, 0)),
            pl.BlockSpec((1, D, F), lambda e, k: (e, 0, 0)),
            pl.BlockSpec((1, F, D), lambda e, k: (e, 0, 0)),
        ],
        out_specs=pl.BlockSpec((ts, D), lambda e, k: (e * kt + k, 0)),
        scratch_shapes=[pltpu.VMEM((ts, D), BF16), pltpu.VMEM((GATHER_BUFS, GATHER_CHUNK, D), BF16),
                        pltpu.SemaphoreType.DMA((GATHER_BUFS,)), pltpu.SMEM((1,), I32)],
        compiler_params=_cparams(2), name="ffn",
    )(starts_e, pos_e, h2, wg, wu, wd)


SLOT_CHUNK = LANES_V7X


def _combine_body(cap, st_ref, en_ref, x_ref, mod_ref, aff_ref, pos_ref, ye_hbm, o_ref, ybuf, sem, c_exp, c_base):
    E = N_EXPERTS
    tc = x_ref.shape[1]
    n_rows = E * cap
    sub = tc // LANES_V7X

    def chunk_copy(row0, idx):
        return pltpu.make_async_copy(ye_hbm.at[pl.ds(row0, SLOT_CHUNK), :], ybuf.at[idx], sem.at[0])

    def add_chunk(e, row0, base, idx):
        c_exp[idx] = e
        c_base[idx] = base
        chunk_copy(row0, idx).start()
        return idx + 1

    def issue_expert(e, idx):
        st, en = st_ref[0, 0, e], en_ref[0, 0, e]
        q0 = st // SLOT_CHUNK
        q1 = jnp.where(en > st, (en + SLOT_CHUNK - 1) // SLOT_CHUNK, q0)

        def one(q, idx):
            row0 = jnp.minimum(e * cap + q * SLOT_CHUNK, n_rows - SLOT_CHUNK)
            return add_chunk(e, row0, row0 - e * cap, idx)
        return lax.fori_loop(q0, q1, one, idx)

    total = lax.fori_loop(0, E, issue_expert, jnp.int32(0))
    total = lax.cond(total % 2 == 1, lambda t: add_chunk(0, 0, NO_SLOT // 2, t), lambda t: t, total)

    def wait_one(j, c):
        chunk_copy(0, j).wait()
        return c
    lax.fori_loop(0, total, wait_one, 0)

    slot_iota = lax.broadcasted_iota(I32, (SLOT_CHUNK, LANES_V7X), 0)

    def weighted_onehot_t(idx):
        e, base = c_exp[idx], c_base[idx]
        parts = []
        for j in range(sub):
            p = pos_ref[j, pl.ds(e, 1), :]
            w = aff_ref[j, pl.ds(e, 1), :]
            parts.append(jnp.where(p == base + slot_iota, w, 0.0))
        return jnp.concatenate(parts, axis=1)

    def pair(pi, acc):
        i0 = 2 * pi
        wt = jnp.concatenate([weighted_onehot_t(i0), weighted_onehot_t(i0 + 1)], axis=0).T
        hi = wt.astype(BF16)
        lo = (wt - hi.astype(F32)).astype(BF16)
        y = jnp.concatenate([ybuf[i0], ybuf[i0 + 1]], axis=0)
        return acc + jnp.dot(hi, y, preferred_element_type=F32) + jnp.dot(lo, y, preferred_element_type=F32)

    acc = lax.fori_loop(0, total // 2, pair, jnp.zeros((tc, D_MODEL), F32))
    m = mod_ref[0]
    o_ref[0] = x_ref[0] + m[:, 5 * D_MODEL:6 * D_MODEL] * acc


def _combine(x1, mod, aff3, ye, pos3, st_t, en_t, cap, tc):
    B, n, D = x1.shape
    nt = n // tc
    sub = tc // LANES_V7X
    mod_b = mod.shape[0]
    E = N_EXPERTS
    max_chunks = E * (tc // SLOT_CHUNK + 1) + 2
    return pl.pallas_call(
        functools.partial(_combine_body, cap),
        out_shape=jax.ShapeDtypeStruct((B, n, D), F32),
        grid=(B, nt),
        in_specs=[
            pl.BlockSpec((1, 1, E), lambda b, i: (b * nt + i, 0, 0), memory_space=pltpu.SMEM),
            pl.BlockSpec((1, 1, E), lambda b, i: (b * nt + i, 0, 0), memory_space=pltpu.SMEM),
            pl.BlockSpec((1, tc, D), lambda b, i: (b, i, 0)),
            pl.BlockSpec((1, 1, 6 * D), (lambda b, i: (b, 0, 0)) if mod_b > 1 else (lambda b, i: (0, 0, 0))),
            pl.BlockSpec((sub, E, LANES_V7X), lambda b, i: (b * nt + i, 0, 0)),
            pl.BlockSpec((sub, E, LANES_V7X), lambda b, i: (b * nt + i, 0, 0)),
            pl.BlockSpec(memory_space=pl.ANY),
        ],
        out_specs=pl.BlockSpec((1, tc, D), lambda b, i: (b, i, 0)),
        scratch_shapes=[pltpu.VMEM((max_chunks, SLOT_CHUNK, D), BF16), pltpu.SemaphoreType.DMA((1,)),
                        pltpu.SMEM((max_chunks,), I32), pltpu.SMEM((max_chunks,), I32)],
        compiler_params=_cparams(2), name="combine",
    )(st_t, en_t, x1, mod, aff3, pos3, ye)


def _expert_choice_ffn(x1, h2, aff3, mod, P, l):
    B, n, D = x1.shape
    E = N_EXPERTS
    n_tok = B * n
    cap = EC_FACTOR * n_tok // E
    nt = n_tok // LANES_V7X
    aff3 = aff3.reshape(nt, E, LANES_V7X)
    pos3, starts3 = _route(aff3, cap)
    starts = starts3[:, :, 0]
    pos_e = jnp.transpose(pos3, (1, 0, 2))
    starts_e = jnp.concatenate([starts.T, jnp.full((E, 1), cap, I32)], axis=1).reshape(E, 1, nt + 1)
    ts = min(512, cap)
    ye = _ffn(h2.reshape(n_tok, D), pos_e, starts_e, P["wgate"][l], P["wup"][l], P["wdown"][l], cap, ts)
    tc = min(512, n)
    sub = tc // LANES_V7X
    st_t = starts[::sub]
    en_t = jnp.concatenate([starts[sub::sub], jnp.full((1, E), cap, I32)], axis=0)
    ntc = n_tok // tc
    return _combine(x1, mod, aff3, ye, pos3, st_t.reshape(ntc, 1, E), en_t.reshape(ntc, 1, E), cap, tc)


def _layer(x, mod, P, l, ctx, rope_tabs):
    B, n, _ = x.shape
    latent = ctx is not None
    t_tile = 256
    outs = _proj(x, mod, P, l, rope_tabs if latent else None, not latent, t_tile)
    qa, ka, va, qb, kb, vb, qc, kc, vc, qd, kd, vd = outs[:12]
    if latent:
        c_ak, c_av, c_bk, c_bv, c_kc, c_vc, c_dk, c_dv = ctx
        tq, tk = 512, 512
    else:
        c_ak = c_av = c_bk = c_bv = c_kc = c_vc = c_dk = c_dv = None
        tq, tk = n, n
    lam_init = _lambda_init(l)
    oa = _attention(qa, ka, va, c_ak, c_av, scale=SCALE_A, kmap=lambda g: g, vmap=lambda g: g, tq=tq, tk=tk,
                    diff=lam_init, lam=P["alam"][l], subln=P["asubln_col"][l], name="attn_a")
    ob = _attention(qb, kb, vb, c_bk, c_bv, scale=SCALE_B, kmap=lambda g: g // 2, vmap=lambda g: g // 2,
                    tq=tq, tk=tk, name="attn_b")
    oc = _attention(qc, kc, vc, c_kc, c_vc, scale=SCALE_C, kmap=lambda g: g, vmap=lambda g: g, tq=tq, tk=tk,
                    name="attn_c")
    od = _attention(qd, kd, vd, c_dk, c_dv, scale=SCALE_D, kmap=lambda g: g // 2, vmap=lambda g: g // 2,
                    tq=tq, tk=tk, window=WINDOW if latent else None, sink=P["dsink"][l], transposed=False,
                    name="attn_d")
    x1, h2, aff3 = _merge(x, mod, P, l, (oa, ob, oc, od), t_tile)
    x2 = _expert_choice_ffn(x1, h2, aff3, mod, P, l)
    return x2, outs[12:]


def _rope_tables(n, dim):
    rows = n // GRID_W
    row = jnp.repeat(jnp.arange(rows, dtype=F32), GRID_W)
    col = jnp.tile(jnp.arange(GRID_W, dtype=F32), rows)
    nf = dim // 4
    inv = ROPE_THETA ** (-jnp.arange(nf, dtype=F32) / nf)
    ar = row[:, None] * inv
    ac = col[:, None] * inv
    ang = jnp.concatenate([ar, ar, ac, ac], axis=-1)
    cos, sin = jnp.cos(ang), jnp.sin(ang)
    first = (jnp.arange(dim) % (2 * nf)) < nf
    sin_a = jnp.where(first[None, :], -sin, 0.0)
    sin_b = jnp.where(first[None, :], 0.0, sin)
    reps = LANES_V7X // dim
    return tuple(jnp.tile(t, (1, reps)) for t in (cos, sin_a, sin_b))


def _prepare_params(ada_w, ada_b, norm1_g, norm2_g, w_in, a_q_norm, a_k_norm, a_lambda, a_subln, b_q_norm,
                    b_k_norm, c_qa_norm, c_kva_norm, c_wq_up, c_wkv_up, c_q_norm, c_k_norm, c_qr_norm,
                    c_kr_norm, d_q_norm, d_k_norm, d_sink, w_branch, w_out, w_router, w_gate, w_up, w_down):
    L = w_in.shape[0]

    def row(a):
        return a[:, None, :].astype(F32)

    def tiled(a, reps):
        return jnp.tile(a, (1, reps))[:, None, :].astype(F32)

    w1 = jnp.concatenate([w_in[:, :, :_CKR_END_ORIG],
                          jnp.zeros((L, D_MODEL, _C_CKR[1] - _C_CKR[0] - ROPE_C), w_in.dtype),
                          w_in[:, :, _CKR_END_ORIG:_W_ORIG_NOGATE]], axis=-1).astype(BF16)
    wq = c_wq_up.reshape(L, Q_LORA, H_C, NOPE_C + ROPE_C)
    wqup = jnp.concatenate([wq[..., :NOPE_C].reshape(L, Q_LORA, H_C * NOPE_C),
                            wq[..., NOPE_C:].reshape(L, Q_LORA, H_C * ROPE_C)], axis=-1).astype(BF16)
    wkv = c_wkv_up.reshape(L, KV_LORA, H_C, NOPE_C + V_C)
    wkvup = jnp.concatenate([wkv[..., :NOPE_C].reshape(L, KV_LORA, H_C * NOPE_C),
                             wkv[..., NOPE_C:].reshape(L, KV_LORA, H_C * V_C)], axis=-1).astype(BF16)
    ckrn = jnp.concatenate([c_kr_norm, jnp.zeros((L, LANES_V7X - ROPE_C), c_kr_norm.dtype)], axis=-1)
    return dict(
        g1=row(norm1_g), g2=row(norm2_g), w1=w1,
        aqn=tiled(a_q_norm, 2 * H_A), akn=tiled(a_k_norm, 2 * H_A),
        bqn=tiled(b_q_norm, H_B), bkn=tiled(b_k_norm, KV_B),
        cqan=row(c_qa_norm), ckvan=row(c_kva_norm), wqup=wqup, wkvup=wkvup,
        cqn=tiled(c_q_norm, H_C), ckn=tiled(c_k_norm, H_C), cqrn=tiled(c_qr_norm, H_C), ckrn=row(ckrn),
        dqn=tiled(d_q_norm, H_D), dkn=tiled(d_k_norm, KV_D),
        alam=a_lambda.astype(F32), asubln_col=a_subln[:, :, None].astype(F32), dsink=d_sink.astype(F32),
        wgates=w_in[:, :, _W_ORIG_NOGATE:].astype(BF16), wbranch=w_branch.astype(BF16), wout=w_out.astype(BF16),
        wrouter_t=jnp.transpose(w_router, (0, 2, 1)).astype(BF16),
        wgate=w_gate.astype(BF16), wup=w_up.astype(BF16), wdown=w_down.astype(BF16),
    )


def _head_major(c):
    return jnp.transpose(c, (0, 2, 1, 3)).astype(BF16)


def _head_major_t(c):
    return jnp.transpose(c, (0, 2, 3, 1)).astype(BF16)


def kernel(x_prompt, x_sample, c, cache_a_k, cache_a_v, cache_b_k, cache_b_v, cache_c_kv, cache_c_kr, cache_d_k, cache_d_v, c_ctx, ada_w, ada_b, norm1_g, norm2_g, w_in, a_q_norm, a_k_norm, a_lambda, a_subln, b_q_norm, b_k_norm, c_qa_norm, c_kva_norm, c_wq_up, c_wkv_up, c_q_norm, c_k_norm, c_qr_norm, c_kr_norm, d_q_norm, d_k_norm, d_sink, w_branch, w_out, w_router, w_gate, w_up, w_down):
    L = w_in.shape[0]
    P = _prepare_params(ada_w, ada_b, norm1_g, norm2_g, w_in, a_q_norm, a_k_norm, a_lambda, a_subln, b_q_norm,
                        b_k_norm, c_qa_norm, c_kva_norm, c_wq_up, c_wkv_up, c_q_norm, c_k_norm, c_qr_norm,
                        c_kr_norm, d_q_norm, d_k_norm, d_sink, w_branch, w_out, w_router, w_gate, w_up, w_down)
    dec_b = c.shape[0]
    rows = 8 * ((1 + dec_b + 7) // 8)
    cond = jnp.concatenate([c_ctx[None, :], c, jnp.zeros((rows - 1 - dec_b, D_MODEL), F32)], axis=0)
    mod = _modulation(cond, ada_w, ada_b)
    mod_ctx = mod[:, 0:1, None, :]
    mod_lat = mod[:, 1:1 + dec_b, None, :]

    x = x_prompt
    caches = []
    for l in range(L):
        x, cache = _layer(x, mod_ctx[l], P, l, None, None)
        caches.append(cache)
    y_prompt = x
    B, n = x_prompt.shape[:2]
    new = [jnp.stack([lc[j] for lc in caches], axis=1) for j in range(8)]
    new_a_k = new[0].reshape(B, L, n, 2 * H_A, HD_A)
    new_a_v = new[1].reshape(B, L, n, H_A, 2 * HD_A)
    new_b_k = new[2].reshape(B, L, n, KV_B, HD_B)
    new_b_v = new[3].reshape(B, L, n, KV_B, HD_B)
    new_c_kv, new_c_kr = new[4], new[5]
    new_d_k = new[6].reshape(B, L, n, KV_D, HD_D)
    new_d_v = new[7].reshape(B, L, n, KV_D, HD_D)

    n_lat = x_sample.shape[1]
    rope_tabs = _rope_tables(n_lat, HD_A) + _rope_tables(n_lat, ROPE_C)
    x = x_sample
    for l in range(L):
        kc_ctx, vc_ctx = _mla_cache(cache_c_kv[:, l], cache_c_kr[:, l], P["wkvup"][l], P["ckn"][l])
        ctx = (_head_major(cache_a_k[:, l]), _head_major_t(cache_a_v[:, l]),
               _head_major(cache_b_k[:, l]), _head_major_t(cache_b_v[:, l]),
               kc_ctx, vc_ctx,
               _head_major(cache_d_k[:, l]), _head_major(cache_d_v[:, l]))
        x, _ = _layer(x, mod_lat[l], P, l, ctx, rope_tabs)
    y_sample = x
    return (y_prompt, y_sample, new_a_k, new_a_v, new_b_k, new_b_v, new_c_kv, new_c_kr, new_d_k, new_d_v)
```
